```python
import math
import jax, jax.numpy as jnp
from jax import lax
import numpy as np

D_MODEL = 1024
BATCH = 8
SEQ = 2048
DEPTH = 2
DEC_BATCH = 128
DEC_SEQ = 8
PAST_LEN = 16384
PAGE_SIZE = 128

D_MIX = D_MODEL
D_CONV = D_MIX // 2
CONV_GROUPS = 8
CONV_W = 31
D_V = D_MIX - D_CONV
GLA_HEADS = 4
D_K = D_V // 2
HEAD_K = D_K // GLA_HEADS
HEAD_V = D_V // GLA_HEADS
GATE_RANK = 16
GATE_NORM = 16.0
CHUNK = 64
EPS = 1e-6
SPLITS_SIZES = (D_CONV, D_CONV, D_CONV, D_K, D_K, D_V, D_V, GATE_RANK)
D_IN = sum(SPLITS_SIZES)
SPLIT_POINTS = tuple(int(s) for s in np.cumsum(SPLITS_SIZES)[:-1])

kernel_name = "hymba_conformer_conv_gla_decoder_step"


def rms_norm(x, g):
    xf = x.astype(jnp.float32)
    y = xf * lax.rsqrt(jnp.mean(xf * xf, axis=-1, keepdims=True) + EPS)
    return (y * g.astype(jnp.float32)).astype(x.dtype)


def group_layer_norm(x, g, b):
    shp = x.shape
    xf = x.astype(jnp.float32).reshape(shp[:-1] + (CONV_GROUPS, shp[-1] // CONV_GROUPS))
    mu = jnp.mean(xf, axis=-1, keepdims=True)
    var = jnp.mean(jnp.square(xf - mu), axis=-1, keepdims=True)
    y = ((xf - mu) * lax.rsqrt(var + EPS)).reshape(shp)
    return (y * g.astype(jnp.float32) + b.astype(jnp.float32)).astype(x.dtype)


def depthwise_causal_conv(u, buf, w, b):
    full = jnp.concatenate([buf.astype(u.dtype), u], axis=1)
    y = lax.conv_general_dilated(full, w[:, None, :].astype(u.dtype), window_strides=(1,),
                                 padding='VALID', dimension_numbers=('NWC', 'WIO', 'NWC'),
                                 feature_group_count=u.shape[-1])
    return y + b.astype(u.dtype), full[:, -(CONV_W - 1):, :]


def gla_chunked(q, k, v, g, S0):
    B, T, H, dk = q.shape
    c = math.gcd(T, CHUNK)
    n = T // c

    def to_chunks(a):
        return a.astype(jnp.float32).reshape(B, n, c, H, a.shape[-1]).transpose(1, 0, 3, 2, 4)

    qc = to_chunks(q) * (dk ** -0.5)
    kc, vc, gc = to_chunks(k), to_chunks(v), to_chunks(g)
    mask = jnp.tril(jnp.ones((c, c), dtype=bool))

    def step(S, inp):
        qi, ki, vi, gi = inp
        bcum = jnp.cumsum(gi, axis=-2)
        q_t = qi * jnp.exp(bcum)
        k_t = ki * jnp.exp(-bcum)
        attn = jnp.where(mask, jnp.einsum('bhtd,bhsd->bhts', q_t, k_t), 0.0)
        o = jnp.einsum('bhtd,bhdv->bhtv', q_t, S) + jnp.einsum('bhts,bhsv->bhtv', attn, vi)
        b_last = bcum[:, :, -1, :]
        k_dec = ki * jnp.exp(b_last[:, :, None, :] - bcum)
        S = S * jnp.exp(b_last)[..., None] + jnp.einsum('bhsd,bhsv->bhdv', k_dec, vi)
        return S, o

    S, o = lax.scan(step, S0.astype(jnp.float32), (qc, kc, vc, gc))
    o = o.transpose(1, 0, 3, 2, 4).reshape(B, T, H, v.shape[-1])
    return o, S


def mixer_layer(x, conv_buf, S0, norm_g, w_in, w_alpha, b_alpha, conv_w, conv_b,
                cn_g, cn_b, w_pw, b_pw, gla_g, w_out):
    B, T, _ = x.shape
    h = rms_norm(x, norm_g)
    proj = jnp.einsum('btd,de->bte', h, w_in.astype(h.dtype))
    a, gl, zc, q, k, v, zg, lr = jnp.split(proj, SPLIT_POINTS, axis=-1)
    u = a * jax.nn.sigmoid(gl)
    cv, new_buf = depthwise_causal_conv(u, conv_buf, conv_w, conv_b)
    cv = jax.nn.silu(group_layer_norm(cv, cn_g, cn_b))
    cv = jnp.einsum('btc,ce->bte', cv, w_pw.astype(cv.dtype)) + b_pw.astype(cv.dtype)
    cv = cv * jax.nn.silu(zc)
    gk = jax.nn.log_sigmoid((jnp.einsum('btr,rk->btk', lr, w_alpha.astype(lr.dtype))
                             + b_alpha.astype(lr.dtype)).astype(jnp.float32)) / GATE_NORM
    o, S = gla_chunked(q.reshape(B, T, GLA_HEADS, HEAD_K), k.reshape(B, T, GLA_HEADS, HEAD_K),
                       v.reshape(B, T, GLA_HEADS, HEAD_V), gk.reshape(B, T, GLA_HEADS, HEAD_K), S0)
    o = o * lax.rsqrt(jnp.mean(o * o, axis=-1, keepdims=True) + EPS) * gla_g.astype(jnp.float32)
    o = o.reshape(B, T, D_V).astype(x.dtype) * jax.nn.silu(zg)
    out = jnp.einsum('bte,ed->btd', jnp.concatenate([cv, o], axis=-1), w_out.astype(x.dtype))
    return x + out, new_buf, S


def setup_inputs(seed: int = 0) -> dict:
    key = jax.random.key(seed)
    ks = jax.random.split(key, 20)
    f = jnp.float32
    nrm = lambda k_, shp, s: jax.random.normal(k_, shp, f) * s
    return {
        "x_prompt": nrm(ks[0], (BATCH, SEQ, D_MODEL), 1.0),
        "x_sample": nrm(ks[1], (DEC_BATCH, DEC_SEQ, D_MODEL), 1.0),
        "cache_conv": nrm(ks[2], (DEPTH, DEC_BATCH, CONV_W - 1, D_CONV), 0.5),
        "state_gla": nrm(ks[3], (DEPTH, DEC_BATCH, GLA_HEADS, HEAD_K, HEAD_V), 0.5),
        "norm_g": 1.0 + nrm(ks[4], (DEPTH, D_MODEL), 0.02),
        "w_in": nrm(ks[5], (DEPTH, D_MODEL, D_IN), D_MODEL ** -0.5),
        "w_alpha": nrm(ks[6], (DEPTH, GATE_RANK, D_K), GATE_RANK ** -0.5),
        "b_alpha": nrm(ks[7], (DEPTH, D_K), 0.1),
        "conv_w": nrm(ks[8], (DEPTH, CONV_W, D_CONV), CONV_W ** -0.5),
        "conv_b": nrm(ks[9], (DEPTH, D_CONV), 0.02),
        "cn_g": 1.0 + nrm(ks[10], (DEPTH, D_CONV), 0.02),
        "cn_b": nrm(ks[11], (DEPTH, D_CONV), 0.02),
        "w_pw": nrm(ks[12], (DEPTH, D_CONV, D_CONV), D_CONV ** -0.5),
        "b_pw": nrm(ks[13], (DEPTH, D_CONV), 0.02),
        "gla_g": 1.0 + nrm(ks[14], (DEPTH, HEAD_V), 0.02),
        "w_out": nrm(ks[15], (DEPTH, D_MIX, D_MODEL), D_MIX ** -0.5),
        "final_g": 1.0 + nrm(ks[16], (D_MODEL,), 0.02),
    }


def reference(x_prompt, x_sample, cache_conv, state_gla, norm_g, w_in, w_alpha, b_alpha,
              conv_w, conv_b, cn_g, cn_b, w_pw, b_pw, gla_g, w_out, final_g):
    hp, hs = x_prompt, x_sample
    conv_p, gla_p, conv_s, gla_s = [], [], [], []
    for l in range(DEPTH):
        lw = (norm_g[l], w_in[l], w_alpha[l], b_alpha[l], conv_w[l], conv_b[l],
              cn_g[l], cn_b[l], w_pw[l], b_pw[l], gla_g[l], w_out[l])
        buf0 = jnp.zeros((BATCH, CONV_W - 1, D_CONV), hp.dtype)
        S0 = jnp.zeros((BATCH, GLA_HEADS, HEAD_K, HEAD_V), jnp.float32)
        hp, bp, Sp = mixer_layer(hp, buf0, S0, *lw)
        hs, bs, Ss = mixer_layer(hs, cache_conv[l], state_gla[l], *lw)
        conv_p.append(bp); gla_p.append(Sp); conv_s.append(bs); gla_s.append(Ss)
    y_prompt = rms_norm(hp, final_g)
    y_sample = rms_norm(hs, final_g)
    return (y_prompt, y_sample, jnp.stack(conv_p), jnp.stack(gla_p), jnp.stack(conv_s), jnp.stack(gla_s))
```

```python
import functools

import jax
import jax.numpy as jnp
import numpy as np
from jax import lax
from jax.experimental import pallas as pl
from jax.experimental.pallas import tpu as pltpu

F32 = jnp.float32
BF16 = jnp.bfloat16

D_MODEL = 1024
D_CONV = 512
CONV_GROUPS = 8
CONV_W = 31
HIST = CONV_W - 1
D_V = 512
HEADS = 4
D_K = 256
HEAD_K = 64
HEAD_V = 128
GATE_RANK = 16
GATE_NORM = 16.0
EPS = 1e-6

LANE = 128
SUBLANE = 8
HIST_PAD = 32
N_LANE_GROUPS = D_CONV // LANE
LR_PAD = LANE

OFF_A, OFF_GL, OFF_ZC = 0, 512, 1024
OFF_Q, OFF_K, OFF_V, OFF_ZG, OFF_LR = 1536, 1792, 2048, 2560, 3072
D_IN = OFF_LR + GATE_RANK
D_IN_PAD = OFF_LR + LR_PAD

PROMPT_TILE = 512
GLA_CHUNK = 128
SAMPLE_SEQS = 16
CONV_UNROLL = 8

VMEM_LIMIT = 56 * 1024 * 1024


def _dot(a, b):
    return jnp.dot(a, b, preferred_element_type=F32)


def _dot_nt(a, b):
    return lax.dot_general(a, b, (((1,), (1,)), ((), ())), preferred_element_type=F32)


def _sigmoid(x):
    return 1.0 / (1.0 + jnp.exp(-x))


def _silu(x):
    return x * _sigmoid(x)


def _log_sigmoid(x):
    return jnp.minimum(x, 0.0) - jnp.log(1.0 + jnp.exp(-jnp.abs(x)))


def _rms_rows(x, g):
    ms = jnp.mean(x * x, axis=-1, keepdims=True)
    return x * lax.rsqrt(ms + EPS) * g


def _split3(g):
    hi = g.astype(BF16)
    r1 = g - hi.astype(F32)
    mid = r1.astype(BF16)
    lo = (r1 - mid.astype(F32)).astype(BF16)
    return hi, mid, lo


def _sum_matmul(m, g):
    hi, mid, lo = _split3(g)
    return _dot(m, hi) + _dot(m, mid) + _dot(m, lo)


def _conv_post(cv, hb, win_ref, cb_ref, cng_ref, cnb_ref, gm_ref, wpw_ref, bpw_ref):
    cv = cv + cb_ref[...]
    gm = gm_ref[...]
    mu = _dot(cv.astype(BF16), gm)
    d = cv - mu
    var = _dot((d * d).astype(BF16), gm)
    yn = d * lax.rsqrt(var + EPS) * cng_ref[...] + cnb_ref[...]
    pw = _dot(_silu(yn).astype(BF16), wpw_ref[...]) + bpw_ref[...]
    zc = _dot(hb, win_ref[:, OFF_ZC:OFF_ZC + D_CONV])
    return pw * _silu(zc)


def _gate_log(hb, win_ref, wal_ref, bal_ref):
    lr = _dot(hb, win_ref[:, OFF_LR:OFF_LR + LR_PAD])
    z = _dot(lr.astype(BF16), wal_ref[...]) + bal_ref[...]
    return _log_sigmoid(z) * (1.0 / GATE_NORM)


def _gla_post(o, hb, win_ref, gg_ref):
    zg = _dot(hb, win_ref[:, OFF_ZG:OFF_ZG + D_V])
    parts = []
    for h in range(HEADS):
        oh = o[:, h * HEAD_V:(h + 1) * HEAD_V]
        ms = jnp.mean(oh * oh, axis=-1, keepdims=True)
        parts.append(oh * lax.rsqrt(ms + EPS) * gg_ref[...])
    return jnp.concatenate(parts, axis=-1) * _silu(zg)


def _finish(x, cat, wout_ref, fg_ref, final):
    y = x + _dot(cat, wout_ref[...])
    if final:
        y = _rms_rows(y, fg_ref[...])
    return y


def _prompt_kernel(x_ref, ng_ref, win_ref, wal_ref, bal_ref, cwb_ref, cb_ref, cng_ref, cnb_ref,
                   gm_ref, wpw_ref, bpw_ref, gg_ref, wout_ref, fg_ref, ltri_ref,
                   y_ref, buf_ref, sout_ref,
                   ubuf, cvs, qs, ks, vs, gs, os_, cat, sst, *, tt, final):
    t = pl.program_id(1)

    @pl.when(t == 0)
    def _():
        ubuf[:, 0:HIST_PAD, :] = jnp.zeros((N_LANE_GROUPS, HIST_PAD, LANE), F32)
        sst[...] = jnp.zeros_like(sst)

    x = x_ref[0]
    hb = _rms_rows(x, ng_ref[...]).astype(BF16)

    a = _dot(hb, win_ref[:, OFF_A:OFF_A + D_CONV])
    gl = _dot(hb, win_ref[:, OFF_GL:OFF_GL + D_CONV])
    u = a * _sigmoid(gl)
    for g in range(N_LANE_GROUPS):
        ubuf[g, HIST_PAD:HIST_PAD + tt, :] = u[:, g * LANE:(g + 1) * LANE]

    base = HIST_PAD - HIST
    seg = tt // SUBLANE
    for g in range(N_LANE_GROUPS):
        w = [cwb_ref[k, :, g * LANE:(g + 1) * LANE] for k in range(CONV_W)]

        def conv_body(ib, carry, g=g, w=w):
            i0 = ib * CONV_UNROLL
            accs = [ubuf[g, pl.ds(i0 + (ii + base), SUBLANE, stride=seg), :] * w[0]
                    for ii in range(CONV_UNROLL)]
            for k in range(1, CONV_W):
                for ii in range(CONV_UNROLL):
                    accs[ii] = accs[ii] + ubuf[g, pl.ds(i0 + (ii + base + k), SUBLANE, stride=seg), :] * w[k]
            for ii in range(CONV_UNROLL):
                cvs[g, pl.ds(i0 + ii, SUBLANE, stride=seg), :] = accs[ii]
            return carry

        lax.fori_loop(0, seg // CONV_UNROLL, conv_body, 0)

    @pl.when(t == pl.num_programs(1) - 1)
    def _():
        buf_ref[0] = jnp.concatenate(
            [ubuf[g, tt + HIST_PAD - HIST:tt + HIST_PAD, :] for g in range(N_LANE_GROUPS)], axis=-1)

    ubuf[:, 0:HIST_PAD, :] = ubuf[:, tt:tt + HIST_PAD, :]

    cv = jnp.concatenate([cvs[g] for g in range(N_LANE_GROUPS)], axis=-1)
    cvo = _conv_post(cv, hb, win_ref, cb_ref, cng_ref, cnb_ref, gm_ref, wpw_ref, bpw_ref)
    cat[:, 0:D_CONV] = cvo.astype(BF16)

    qs[...] = _dot(hb, win_ref[:, OFF_Q:OFF_Q + D_K]) * (HEAD_K ** -0.5)
    ks[...] = _dot(hb, win_ref[:, OFF_K:OFF_K + D_K])
    vs[...] = _dot(hb, win_ref[:, OFF_V:OFF_V + D_V]).astype(BF16)
    gs[...] = _gate_log(hb, win_ref, wal_ref, bal_ref)

    c = GLA_CHUNK
    half = c // 2
    row = lax.broadcasted_iota(jnp.int32, (c, c), 0)
    col = lax.broadcasted_iota(jnp.int32, (c, c), 1)
    causal = col <= row
    head_lo = col < HEAD_K
    ltri = ltri_ref[...]

    def chunk_body(i, carry):
        r0 = pl.multiple_of(i * c, c)
        rows = pl.ds(r0, c)
        gcum = _sum_matmul(ltri, gs[rows, :])
        q = qs[rows, :]
        k = ks[rows, :]
        v = vs[rows, :]
        mid = gcum[half - 1:half, :]
        xq = q * jnp.exp(gcum - mid)
        yk = (k * jnp.exp(mid - gcum)).astype(BF16)
        qg = q * jnp.exp(gcum)
        gt = gcum.T
        glast = gt[:, c - 1:c]
        kdt = (k.T * jnp.exp(glast - gt)).astype(BF16)
        dcol = jnp.exp(glast)
        s_old = sst[...]
        s_bf = s_old.astype(BF16)
        outs = []
        for h in range(HEADS):
            j, hi = divmod(h, 2)
            pair = slice(j * LANE, (j + 1) * LANE)
            hm = head_lo if hi == 0 else jnp.logical_not(head_lo)
            xm = jnp.where(hm, xq[:, pair], 0.0).astype(BF16)
            qm = jnp.where(hm, qg[:, pair], 0.0).astype(BF16)
            att = jnp.where(causal, _dot_nt(xm, yk[:, pair]), 0.0).astype(BF16)
            vh = v[:, h * HEAD_V:(h + 1) * HEAD_V]
            outs.append(_dot(att, vh) + _dot(qm, s_bf[pair, :]))
            hr = slice(h * HEAD_K, (h + 1) * HEAD_K)
            sst[hr, :] = dcol[hr, :] * s_old[hr, :] + _dot(kdt[hr, :], vh)
        os_[rows, :] = jnp.concatenate(outs, axis=-1)
        return carry

    lax.fori_loop(0, tt // c, chunk_body, 0)

    @pl.when(t == pl.num_programs(1) - 1)
    def _():
        sout_ref[0] = sst[...]

    cat[:, D_CONV:] = _gla_post(os_[...], hb, win_ref, gg_ref).astype(BF16)
    y_ref[0] = _finish(x, cat[...], wout_ref, fg_ref, final)


def _const_spec(shape):
    return pl.BlockSpec(shape, lambda *_: (0,) * len(shape), pipeline_mode=pl.Buffered(1))


def _prompt_layer(x, lw, final):
    b, t, _ = x.shape
    tt = PROMPT_TILE
    assert t % tt == 0 and tt % GLA_CHUNK == 0 and tt >= HIST_PAD
    consts = lw + (_lower_tri(GLA_CHUNK),)
    in_specs = [pl.BlockSpec((1, tt, D_MODEL), lambda i, j: (i, j, 0))]
    in_specs += [_const_spec(c.shape) for c in consts]
    out_shape = (jax.ShapeDtypeStruct((b, t, D_MODEL), F32),
                 jax.ShapeDtypeStruct((b, HIST, D_CONV), F32),
                 jax.ShapeDtypeStruct((b, D_K, HEAD_V), F32))
    out_specs = (pl.BlockSpec((1, tt, D_MODEL), lambda i, j: (i, j, 0)),
                 pl.BlockSpec((1, HIST, D_CONV), lambda i, j: (i, 0, 0)),
                 pl.BlockSpec((1, D_K, HEAD_V), lambda i, j: (i, 0, 0)))
    scratch = [pltpu.VMEM((N_LANE_GROUPS, tt + HIST_PAD, LANE), F32),
               pltpu.VMEM((N_LANE_GROUPS, tt, LANE), F32),
               pltpu.VMEM((tt, D_K), F32),
               pltpu.VMEM((tt, D_K), F32),
               pltpu.VMEM((tt, D_V), BF16),
               pltpu.VMEM((tt, D_K), F32),
               pltpu.VMEM((tt, D_V), F32),
               pltpu.VMEM((tt, D_MODEL), BF16),
               pltpu.VMEM((D_K, HEAD_V), F32)]
    y, buf, s = pl.pallas_call(
        functools.partial(_prompt_kernel, tt=tt, final=final),
        grid=(b, t // tt),
        in_specs=in_specs, out_specs=out_specs, out_shape=out_shape,
        scratch_shapes=scratch,
        compiler_params=pltpu.CompilerParams(
            dimension_semantics=("arbitrary", "arbitrary"), vmem_limit_bytes=VMEM_LIMIT),
        name="prompt_layer",
    )(x, *consts)
    return y, buf, s.reshape(b, HEADS, HEAD_K, HEAD_V)


def _sample_kernel(x_ref, cache_ref, s0_ref, ng_ref, win_ref, wal_ref, bal_ref, cwb_ref, cb_ref,
                   cng_ref, cnb_ref, gm_ref, wpw_ref, bpw_ref, gg_ref, wout_ref, fg_ref,
                   ltri_ref, ball_ref,
                   y_ref, buf_ref, sout_ref, full, *, nb, ts, final):
    r = nb * ts
    x = x_ref[...].reshape(r, D_MODEL)
    hb = _rms_rows(x, ng_ref[...]).astype(BF16)

    a = _dot(hb, win_ref[:, OFF_A:OFF_A + D_CONV])
    gl = _dot(hb, win_ref[:, OFF_GL:OFF_GL + D_CONV])
    full[:, 0:HIST, :] = cache_ref[...]
    full[:, HIST:HIST + ts, :] = (a * _sigmoid(gl)).reshape(nb, ts, D_CONV)
    acc = full[:, 0:ts, :] * cwb_ref[0]
    for k in range(1, CONV_W):
        acc = acc + full[:, k:k + ts, :] * cwb_ref[k]
    buf_ref[...] = full[:, ts:ts + HIST, :]
    cvo = _conv_post(acc.reshape(r, D_CONV), hb, win_ref, cb_ref, cng_ref, cnb_ref, gm_ref,
                     wpw_ref, bpw_ref)

    q = _dot(hb, win_ref[:, OFF_Q:OFF_Q + D_K]) * (HEAD_K ** -0.5)
    k = _dot(hb, win_ref[:, OFF_K:OFF_K + D_K])
    v = _dot(hb, win_ref[:, OFF_V:OFF_V + D_V]).astype(BF16)
    g = _gate_log(hb, win_ref, wal_ref, bal_ref)
    gcum = _sum_matmul(ltri_ref[...], g)
    gtot = _sum_matmul(ball_ref[...], g)
    xq = q * jnp.exp(gcum)
    yk = (k * jnp.exp(-gcum)).astype(BF16)
    kdt = (k * jnp.exp(gtot - gcum)).T.astype(BF16)
    gt = g.T

    row = lax.broadcasted_iota(jnp.int32, (r, r), 0)
    col = lax.broadcasted_iota(jnp.int32, (r, r), 1)
    causal = jnp.logical_and(col <= row, (col // ts) == (row // ts))
    head_lo = lax.broadcasted_iota(jnp.int32, (r, LANE), 1) < HEAD_K
    seq_sel = (lax.broadcasted_iota(jnp.int32, (nb, 1, r), 2) // ts
               == lax.broadcasted_iota(jnp.int32, (nb, 1, r), 0))
    blk = (lax.broadcasted_iota(jnp.int32, (r, nb * LANE), 1) // LANE
           == lax.broadcasted_iota(jnp.int32, (r, nb * LANE), 0) // ts)

    s0 = s0_ref[...]
    s0_bf = s0.astype(BF16)
    gsum = jnp.sum(jnp.where(seq_sel, gt[None, :, :], 0.0), axis=-1, keepdims=True)
    s_new = jnp.exp(gsum) * s0

    outs = []
    kvs = []
    for h in range(HEADS):
        j, hi = divmod(h, 2)
        pair = slice(j * LANE, (j + 1) * LANE)
        hm = head_lo if hi == 0 else jnp.logical_not(head_lo)
        xm = jnp.where(hm, xq[:, pair], 0.0).astype(BF16)
        att = jnp.where(causal, _dot_nt(xm, yk[:, pair]), 0.0).astype(BF16)
        vh = v[:, h * HEAD_V:(h + 1) * HEAD_V]
        xblk = jnp.where(blk, jnp.concatenate([xm] * nb, axis=-1), jnp.zeros((), BF16))
        s_pair = s0_bf[:, pair, :].reshape(nb * LANE, HEAD_V)
        outs.append(_dot(att, vh) + _dot(xblk, s_pair))
        kh = kdt[h * HEAD_K:(h + 1) * HEAD_K, :]
        kblk = jnp.where(seq_sel, kh[None, :, :], jnp.zeros((), BF16)).reshape(nb * HEAD_K, r)
        kvs.append(_dot(kblk, vh).reshape(nb, HEAD_K, HEAD_V))
    sout_ref[...] = s_new + jnp.concatenate(kvs, axis=1)

    o = jnp.concatenate(outs, axis=-1)
    cat = jnp.concatenate([cvo.astype(BF16), _gla_post(o, hb, win_ref, gg_ref).astype(BF16)],
                          axis=-1)
    y_ref[...] = _finish(x, cat, wout_ref, fg_ref, final).reshape(nb, ts, D_MODEL)


def _sample_layer(x, cache, s0, lw, final):
    b, ts, _ = x.shape
    nb = SAMPLE_SEQS
    assert b % nb == 0 and ts == SUBLANE and nb * ts == LANE
    r = nb * ts
    consts = lw + (_block_tri(r, ts), _block_ones(r, ts))
    s0 = s0.reshape(b, D_K, HEAD_V)
    in_specs = [pl.BlockSpec((nb, ts, D_MODEL), lambda i: (i, 0, 0)),
                pl.BlockSpec((nb, HIST, D_CONV), lambda i: (i, 0, 0)),
                pl.BlockSpec((nb, D_K, HEAD_V), lambda i: (i, 0, 0))]
    in_specs += [_const_spec(c.shape) for c in consts]
    out_shape = (jax.ShapeDtypeStruct((b, ts, D_MODEL), F32),
                 jax.ShapeDtypeStruct((b, HIST, D_CONV), F32),
                 jax.ShapeDtypeStruct((b, D_K, HEAD_V), F32))
    out_specs = (pl.BlockSpec((nb, ts, D_MODEL), lambda i: (i, 0, 0)),
                 pl.BlockSpec((nb, HIST, D_CONV), lambda i: (i, 0, 0)),
                 pl.BlockSpec((nb, D_K, HEAD_V), lambda i: (i, 0, 0)))
    y, buf, s = pl.pallas_call(
        functools.partial(_sample_kernel, nb=nb, ts=ts, final=final),
        grid=(b // nb,),
        in_specs=in_specs, out_specs=out_specs, out_shape=out_shape,
        scratch_shapes=[pltpu.VMEM((nb, HIST + ts + 2, D_CONV), F32)],
        compiler_params=pltpu.CompilerParams(
            dimension_semantics=("arbitrary",), vmem_limit_bytes=VMEM_LIMIT),
        name="sample_layer",
    )(x, cache, s0, *consts)
    return y, buf, s.reshape(b, HEADS, HEAD_K, HEAD_V)


def _lower_tri(n):
    return jnp.asarray(np.tril(np.ones((n, n), np.float32)), BF16)


def _block_tri(n, blk):
    i = np.arange(n)
    m = (i[None, :] <= i[:, None]) & (i[None, :] // blk == i[:, None] // blk)
    return jnp.asarray(m.astype(np.float32), BF16)


def _block_ones(n, blk):
    i = np.arange(n)
    return jnp.asarray((i[None, :] // blk == i[:, None] // blk).astype(np.float32), BF16)


def _group_mean_matrix():
    gsz = D_CONV // CONV_GROUPS
    i = np.arange(D_CONV)
    m = (i[None, :] // gsz == i[:, None] // gsz).astype(np.float32) / gsz
    return jnp.asarray(m, BF16)


def _layer_weights(l, norm_g, w_in, w_alpha, b_alpha, conv_w, conv_b, cn_g, cn_b, w_pw, b_pw,
                   gla_g, w_out, final_g):
    row = lambda a: a.reshape(1, -1).astype(F32)
    win = jnp.pad(w_in[l], ((0, 0), (0, D_IN_PAD - D_IN))).astype(BF16)
    wal = jnp.pad(w_alpha[l], ((0, LR_PAD - GATE_RANK), (0, 0))).astype(BF16)
    cwb = jnp.broadcast_to(conv_w[l][:, None, :], (CONV_W, SUBLANE, D_CONV)).astype(F32)
    return (row(norm_g[l]), win, wal, row(b_alpha[l]), cwb, row(conv_b[l]), row(cn_g[l]),
            row(cn_b[l]), _group_mean_matrix(), w_pw[l].astype(BF16), row(b_pw[l]),
            row(gla_g[l]), w_out[l].astype(BF16), row(final_g))


def kernel(x_prompt, x_sample, cache_conv, state_gla, norm_g, w_in, w_alpha, b_alpha, conv_w,
           conv_b, cn_g, cn_b, w_pw, b_pw, gla_g, w_out, final_g):
    depth = w_in.shape[0]
    hp, hs = x_prompt, x_sample
    conv_p, gla_p, conv_s, gla_s = [], [], [], []
    for l in range(depth):
        lw = _layer_weights(l, norm_g, w_in, w_alpha, b_alpha, conv_w, conv_b, cn_g, cn_b,
                            w_pw, b_pw, gla_g, w_out, final_g)
        final = l == depth - 1
        hp, bp, sp = _prompt_layer(hp, lw, final)
        hs, bs, ss = _sample_layer(hs, cache_conv[l], state_gla[l], lw, final)
        conv_p.append(bp); gla_p.append(sp); conv_s.append(bs); gla_s.append(ss)
    return (hp, hs, jnp.stack(conv_p), jnp.stack(gla_p), jnp.stack(conv_s), jnp.stack(gla_s))
```

```python
import functools

import jax
import jax.numpy as jnp
import numpy as np
from jax import lax
from jax.experimental import pallas as pl
from jax.experimental.pallas import tpu as pltpu

F32 = jnp.float32
BF16 = jnp.bfloat16

D_MODEL = 1024
D_CONV = 512
CONV_GROUPS = 8
CONV_W = 31
HIST = CONV_W - 1
D_V = 512
HEADS = 4
D_K = 256
HEAD_K = 64
HEAD_V = 128
GATE_RANK = 16
GATE_NORM = 16.0
EPS = 1e-6

LANE = 128
SUBLANE = 8
HIST_PAD = 32
N_LANE_GROUPS = D_CONV // LANE
LR_PAD = LANE

OFF_A, OFF_GL, OFF_ZC = 0, 512, 1024
OFF_Q, OFF_K, OFF_V, OFF_ZG, OFF_LR = 1536, 1792, 2048, 2560, 3072
D_IN = OFF_LR + GATE_RANK
D_IN_PAD = OFF_LR + LR_PAD
PROJ_COLS = OFF_LR - OFF_ZC
STAGE_COLS = 256

PROMPT_TILE = 512
PROMPT_SEG = PROMPT_TILE // SUBLANE
UHALO0 = SUBLANE
UDATA0 = UHALO0 + HIST_PAD
USEG = UDATA0 + PROMPT_SEG
CSEG = PROMPT_SEG + SUBLANE
assert USEG % 32 == 8 and CSEG % 32 == 8
GLA_CHUNK = 128
SAMPLE_SEQS = 16
CONV_UNROLL = 8

VMEM_LIMIT = 56 * 1024 * 1024


def _dot(a, b):
    return jnp.dot(a, b, preferred_element_type=F32)


def _dot_nt(a, b):
    return lax.dot_general(a, b, (((1,), (1,)), ((), ())), preferred_element_type=F32)


def _sigmoid(x):
    return 1.0 / (1.0 + jnp.exp(-x))


def _silu(x):
    return x * _sigmoid(x)


def _log_sigmoid(x):
    return jnp.minimum(x, 0.0) - jnp.log(1.0 + jnp.exp(-jnp.abs(x)))


def _rms_rows(x, g):
    ms = jnp.mean(x * x, axis=-1, keepdims=True)
    return x * lax.rsqrt(ms + EPS) * g


def _split3(g):
    hi = g.astype(BF16)
    r1 = g - hi.astype(F32)
    mid = r1.astype(BF16)
    lo = (r1 - mid.astype(F32)).astype(BF16)
    return hi, mid, lo


def _sum_matmul(m, g):
    hi, mid, lo = _split3(g)
    return _dot(m, hi) + _dot(m, mid) + _dot(m, lo)


def _conv_post(cv, zc, cb_ref, cng_ref, cnb_ref, gm_ref, wpw_ref, bpw_ref):
    cv = cv + cb_ref[...]
    gm = gm_ref[...]
    mu = _dot(cv.astype(BF16), gm)
    d = cv - mu
    var = _dot((d * d).astype(BF16), gm)
    yn = d * lax.rsqrt(var + EPS) * cng_ref[...] + cnb_ref[...]
    pw = _dot(_silu(yn).astype(BF16), wpw_ref[...]) + bpw_ref[...]
    return pw * _silu(zc)


def _gate_log(hb, win_ref, wal_ref, bal_ref):
    lr = _dot(hb, win_ref[:, OFF_LR:OFF_LR + LR_PAD])
    z = _dot(lr.astype(BF16), wal_ref[...]) + bal_ref[...]
    return _log_sigmoid(z) * (1.0 / GATE_NORM)


def _gla_post(o, zg, gg_ref):
    parts = []
    for h in range(HEADS):
        oh = o[:, h * HEAD_V:(h + 1) * HEAD_V]
        ms = jnp.mean(oh * oh, axis=-1, keepdims=True)
        parts.append(oh * lax.rsqrt(ms + EPS) * gg_ref[...])
    return jnp.concatenate(parts, axis=-1) * _silu(zg)


def _finish(x, cat, wout_ref, fg_ref, final):
    y = x + _dot(cat, wout_ref[...])
    if final:
        y = _rms_rows(y, fg_ref[...])
    return y


def _prompt_kernel(x_ref, ng_ref, win_ref, wal_ref, bal_ref, cwb_ref, cb_ref, cng_ref, cnb_ref,
                   gm_ref, wpw_ref, bpw_ref, gg_ref, wout_ref, fg_ref, ltri_ref,
                   y_ref, buf_ref, sout_ref,
                   hbs, ubuf, cvs, proj, os_, cat, sst, *, tt, final):
    t = pl.program_id(1)
    seg = tt // SUBLANE
    blocks_per_group = seg // CONV_UNROLL
    n_stage = PROJ_COLS // STAGE_COLS
    blocks_per_stage = N_LANE_GROUPS * blocks_per_group // n_stage

    last = (SUBLANE - 1) * USEG + USEG - HIST_PAD

    @pl.when(t == 0)
    def _():
        ubuf[:, UHALO0:UDATA0, :] = jnp.zeros((N_LANE_GROUPS, HIST_PAD, LANE), F32)
        sst[...] = jnp.zeros_like(sst)

    @pl.when(t > 0)
    def _():
        ubuf[:, UHALO0:UDATA0, :] = ubuf[:, last:last + HIST_PAD, :]

    x = x_ref[0]
    hb = _rms_rows(x, ng_ref[...]).astype(BF16)
    hbs[...] = hb

    a = _dot(hb, win_ref[:, OFF_A:OFF_A + D_CONV])
    gl = _dot(hb, win_ref[:, OFF_GL:OFF_GL + D_CONV])
    u = a * _sigmoid(gl)
    for g in range(N_LANE_GROUPS):
        for j in range(SUBLANE):
            ug = u[:, g * LANE:(g + 1) * LANE]
            ubuf[g, j * USEG + UDATA0:(j + 1) * USEG, :] = ug[j * seg:(j + 1) * seg, :]
            if j > 0:
                ubuf[g, j * USEG + UHALO0:j * USEG + UDATA0, :] = ug[j * seg - HIST_PAD:j * seg, :]
    gk = _gate_log(hb, win_ref, wal_ref, bal_ref)

    base = UDATA0 - HIST

    def stage(n, carry):
        blk0 = n * blocks_per_stage
        g = blk0 // blocks_per_group
        lane0 = pl.multiple_of(g * LANE, LANE)
        w = [cwb_ref[k, :, pl.ds(lane0, LANE)] for k in range(CONV_W)]
        for jb in range(blocks_per_stage):
            i0 = lax.rem(blk0 + jb, blocks_per_group) * CONV_UNROLL
            accs = [None] * CONV_UNROLL
            for m in range(CONV_UNROLL + CONV_W - 1):
                um = ubuf[g, pl.ds(i0 + (base + m), SUBLANE, stride=USEG), :]
                for ii in range(CONV_UNROLL):
                    k = m - ii
                    if 0 <= k < CONV_W:
                        accs[ii] = um * w[k] if accs[ii] is None else accs[ii] + um * w[k]
            for ii in range(CONV_UNROLL):
                cvs[g, pl.ds(i0 + ii, SUBLANE, stride=CSEG), :] = accs[ii]
        col0 = pl.multiple_of(n * STAGE_COLS, STAGE_COLS)
        proj[:, pl.ds(col0, STAGE_COLS)] = _dot(
            hbs[...], win_ref[:, pl.ds(OFF_ZC + col0, STAGE_COLS)])
        return carry

    lax.fori_loop(0, n_stage, stage, 0)

    @pl.when(t == pl.num_programs(1) - 1)
    def _():
        buf_ref[0] = jnp.concatenate(
            [ubuf[g, last + HIST_PAD - HIST:last + HIST_PAD, :] for g in range(N_LANE_GROUPS)],
            axis=-1)

    cv = jnp.concatenate(
        [jnp.concatenate([cvs[g, j * CSEG:j * CSEG + seg, :] for j in range(SUBLANE)], axis=0)
         for g in range(N_LANE_GROUPS)], axis=-1)
    zc = proj[:, OFF_ZC - OFF_ZC:OFF_Q - OFF_ZC]
    cvo = _conv_post(cv, zc, cb_ref, cng_ref, cnb_ref, gm_ref, wpw_ref, bpw_ref)
    cat[:, 0:D_CONV] = cvo.astype(BF16)

    c = GLA_CHUNK
    half = c // 2
    row = lax.broadcasted_iota(jnp.int32, (c, c), 0)
    col = lax.broadcasted_iota(jnp.int32, (c, c), 1)
    causal = col <= row
    head_lo = col < HEAD_K
    ltri = ltri_ref[...]

    s_old = sst[...]
    for i in range(tt // c):
        rows = slice(i * c, (i + 1) * c)
        gcum = _sum_matmul(ltri, gk[rows, :])
        q = proj[rows, OFF_Q - OFF_ZC:OFF_K - OFF_ZC] * (HEAD_K ** -0.5)
        k = proj[rows, OFF_K - OFF_ZC:OFF_V - OFF_ZC]
        v = proj[rows, OFF_V - OFF_ZC:OFF_ZG - OFF_ZC].astype(BF16)
        mid = gcum[half - 1:half, :]
        xq = q * jnp.exp(gcum - mid)
        yk = (k * jnp.exp(mid - gcum)).astype(BF16)
        qg = q * jnp.exp(gcum)
        gt = gcum.T
        glast = gt[:, c - 1:c]
        kdt = (k.T * jnp.exp(glast - gt)).astype(BF16)
        dcol = jnp.exp(glast)
        s_bf = s_old.astype(BF16)
        outs = []
        s_new = []
        for h in range(HEADS):
            j, hi = divmod(h, 2)
            pair = slice(j * LANE, (j + 1) * LANE)
            hm = head_lo if hi == 0 else jnp.logical_not(head_lo)
            xm = jnp.where(hm, xq[:, pair], 0.0).astype(BF16)
            qm = jnp.where(hm, qg[:, pair], 0.0).astype(BF16)
            att = jnp.where(causal, _dot_nt(xm, yk[:, pair]), 0.0).astype(BF16)
            vh = v[:, h * HEAD_V:(h + 1) * HEAD_V]
            outs.append(_dot(att, vh) + _dot(qm, s_bf[pair, :]))
            hr = slice(h * HEAD_K, (h + 1) * HEAD_K)
            s_new.append(dcol[hr, :] * s_old[hr, :] + _dot(kdt[hr, :], vh))
        s_old = jnp.concatenate(s_new, axis=0)
        os_[rows, :] = jnp.concatenate(outs, axis=-1)
    sst[...] = s_old

    @pl.when(t == pl.num_programs(1) - 1)
    def _():
        sout_ref[0] = sst[...]

    zg = proj[:, OFF_ZG - OFF_ZC:OFF_LR - OFF_ZC]
    cat[:, D_CONV:] = _gla_post(os_[...], zg, gg_ref).astype(BF16)
    y_ref[0] = _finish(x, cat[...], wout_ref, fg_ref, final)


def _const_spec(shape):
    return pl.BlockSpec(shape, lambda *_: (0,) * len(shape), pipeline_mode=pl.Buffered(1))


def _prompt_layer(x, lw, final):
    b, t, _ = x.shape
    tt = PROMPT_TILE
    assert t % tt == 0 and tt % GLA_CHUNK == 0 and tt >= HIST_PAD
    blocks_per_group = tt // SUBLANE // CONV_UNROLL
    n_stage = PROJ_COLS // STAGE_COLS
    assert tt % (SUBLANE * CONV_UNROLL) == 0 and PROJ_COLS % STAGE_COLS == 0
    assert (N_LANE_GROUPS * blocks_per_group) % n_stage == 0
    assert blocks_per_group % (N_LANE_GROUPS * blocks_per_group // n_stage) == 0
    consts = lw + (_lower_tri(GLA_CHUNK),)
    in_specs = [pl.BlockSpec((1, tt, D_MODEL), lambda i, j: (i, j, 0))]
    in_specs += [_const_spec(c.shape) for c in consts]
    out_shape = (jax.ShapeDtypeStruct((b, t, D_MODEL), F32),
                 jax.ShapeDtypeStruct((b, HIST, D_CONV), F32),
                 jax.ShapeDtypeStruct((b, D_K, HEAD_V), F32))
    out_specs = (pl.BlockSpec((1, tt, D_MODEL), lambda i, j: (i, j, 0)),
                 pl.BlockSpec((1, HIST, D_CONV), lambda i, j: (i, 0, 0)),
                 pl.BlockSpec((1, D_K, HEAD_V), lambda i, j: (i, 0, 0)))
    scratch = [pltpu.VMEM((tt, D_MODEL), BF16),
               pltpu.VMEM((N_LANE_GROUPS, SUBLANE * USEG, LANE), F32),
               pltpu.VMEM((N_LANE_GROUPS, SUBLANE * CSEG, LANE), F32),
               pltpu.VMEM((tt, PROJ_COLS), F32),
               pltpu.VMEM((tt, D_V), F32),
               pltpu.VMEM((tt, D_MODEL), BF16),
               pltpu.VMEM((D_K, HEAD_V), F32)]
    y, buf, s = pl.pallas_call(
        functools.partial(_prompt_kernel, tt=tt, final=final),
        grid=(b, t // tt),
        in_specs=in_specs, out_specs=out_specs, out_shape=out_shape,
        scratch_shapes=scratch,
        compiler_params=pltpu.CompilerParams(
            dimension_semantics=("arbitrary", "arbitrary"), vmem_limit_bytes=VMEM_LIMIT),
        name="prompt_layer",
    )(x, *consts)
    return y, buf, s.reshape(b, HEADS, HEAD_K, HEAD_V)


def _sample_kernel(x_ref, cache_ref, s0_ref, ng_ref, win_ref, wal_ref, bal_ref, cwb_ref, cb_ref,
                   cng_ref, cnb_ref, gm_ref, wpw_ref, bpw_ref, gg_ref, wout_ref, fg_ref,
                   ltri_ref, ball_ref,
                   y_ref, buf_ref, sout_ref, full, *, nb, ts, final):
    r = nb * ts
    x = x_ref[...].reshape(r, D_MODEL)
    hb = _rms_rows(x, ng_ref[...]).astype(BF16)

    a = _dot(hb, win_ref[:, OFF_A:OFF_A + D_CONV])
    gl = _dot(hb, win_ref[:, OFF_GL:OFF_GL + D_CONV])
    full[:, 0:HIST, :] = cache_ref[...]
    full[:, HIST:HIST + ts, :] = (a * _sigmoid(gl)).reshape(nb, ts, D_CONV)
    acc = full[:, 0:ts, :] * cwb_ref[0]
    for k in range(1, CONV_W):
        acc = acc + full[:, k:k + ts, :] * cwb_ref[k]
    buf_ref[...] = full[:, ts:ts + HIST, :]
    zc = _dot(hb, win_ref[:, OFF_ZC:OFF_ZC + D_CONV])
    cvo = _conv_post(acc.reshape(r, D_CONV), zc, cb_ref, cng_ref, cnb_ref, gm_ref,
                     wpw_ref, bpw_ref)

    q = _dot(hb, win_ref[:, OFF_Q:OFF_Q + D_K]) * (HEAD_K ** -0.5)
    k = _dot(hb, win_ref[:, OFF_K:OFF_K + D_K])
    v = _dot(hb, win_ref[:, OFF_V:OFF_V + D_V]).astype(BF16)
    g = _gate_log(hb, win_ref, wal_ref, bal_ref)
    gcum = _sum_matmul(ltri_ref[...], g)
    gtot = _sum_matmul(ball_ref[...], g)
    xq = q * jnp.exp(gcum)
    yk = (k * jnp.exp(-gcum)).astype(BF16)
    kdt = (k * jnp.exp(gtot - gcum)).T.astype(BF16)
    gt = g.T

    row = lax.broadcasted_iota(jnp.int32, (r, r), 0)
    col = lax.broadcasted_iota(jnp.int32, (r, r), 1)
    causal = jnp.logical_and(col <= row, (col // ts) == (row // ts))
    head_lo = lax.broadcasted_iota(jnp.int32, (r, LANE), 1) < HEAD_K
    seq_sel = (lax.broadcasted_iota(jnp.int32, (nb, 1, r), 2) // ts
               == lax.broadcasted_iota(jnp.int32, (nb, 1, r), 0))
    blk = (lax.broadcasted_iota(jnp.int32, (r, nb * LANE), 1) // LANE
           == lax.broadcasted_iota(jnp.int32, (r, nb * LANE), 0) // ts)

    s0 = s0_ref[...]
    s0_bf = s0.astype(BF16)
    gsum = jnp.sum(jnp.where(seq_sel, gt[None, :, :], 0.0), axis=-1, keepdims=True)
    s_new = jnp.exp(gsum) * s0

    outs = []
    kvs = []
    for h in range(HEADS):
        j, hi = divmod(h, 2)
        pair = slice(j * LANE, (j + 1) * LANE)
        hm = head_lo if hi == 0 else jnp.logical_not(head_lo)
        xm = jnp.where(hm, xq[:, pair], 0.0).astype(BF16)
        att = jnp.where(causal, _dot_nt(xm, yk[:, pair]), 0.0).astype(BF16)
        vh = v[:, h * HEAD_V:(h + 1) * HEAD_V]
        xblk = jnp.where(blk, jnp.concatenate([xm] * nb, axis=-1), jnp.zeros((), BF16))
        s_pair = s0_bf[:, pair, :].reshape(nb * LANE, HEAD_V)
        outs.append(_dot(att, vh) + _dot(xblk, s_pair))
        kh = kdt[h * HEAD_K:(h + 1) * HEAD_K, :]
        kblk = jnp.where(seq_sel, kh[None, :, :], jnp.zeros((), BF16)).reshape(nb * HEAD_K, r)
        kvs.append(_dot(kblk, vh).reshape(nb, HEAD_K, HEAD_V))
    sout_ref[...] = s_new + jnp.concatenate(kvs, axis=1)

    o = jnp.concatenate(outs, axis=-1)
    zg = _dot(hb, win_ref[:, OFF_ZG:OFF_ZG + D_V])
    cat = jnp.concatenate([cvo.astype(BF16), _gla_post(o, zg, gg_ref).astype(BF16)], axis=-1)
    y_ref[...] = _finish(x, cat, wout_ref, fg_ref, final).reshape(nb, ts, D_MODEL)


def _sample_layer(x, cache, s0, lw, final):
    b, ts, _ = x.shape
    nb = SAMPLE_SEQS
    assert b % nb == 0 and ts == SUBLANE and nb * ts == LANE
    r = nb * ts
    consts = lw + (_block_tri(r, ts), _block_ones(r, ts))
    s0 = s0.reshape(b, D_K, HEAD_V)
    in_specs = [pl.BlockSpec((nb, ts, D_MODEL), lambda i: (i, 0, 0)),
                pl.BlockSpec((nb, HIST, D_CONV), lambda i: (i, 0, 0)),
                pl.BlockSpec((nb, D_K, HEAD_V), lambda i: (i, 0, 0))]
    in_specs += [_const_spec(c.shape) for c in consts]
    out_shape = (jax.ShapeDtypeStruct((b, ts, D_MODEL), F32),
                 jax.ShapeDtypeStruct((b, HIST, D_CONV), F32),
                 jax.ShapeDtypeStruct((b, D_K, HEAD_V), F32))
    out_specs = (pl.BlockSpec((nb, ts, D_MODEL), lambda i: (i, 0, 0)),
                 pl.BlockSpec((nb, HIST, D_CONV), lambda i: (i, 0, 0)),
                 pl.BlockSpec((nb, D_K, HEAD_V), lambda i: (i, 0, 0)))
    y, buf, s = pl.pallas_call(
        functools.partial(_sample_kernel, nb=nb, ts=ts, final=final),
        grid=(b // nb,),
        in_specs=in_specs, out_specs=out_specs, out_shape=out_shape,
        scratch_shapes=[pltpu.VMEM((nb, HIST + ts + 2, D_CONV), F32)],
        compiler_params=pltpu.CompilerParams(
            dimension_semantics=("arbitrary",), vmem_limit_bytes=VMEM_LIMIT),
        name="sample_layer",
    )(x, cache, s0, *consts)
    return y, buf, s.reshape(b, HEADS, HEAD_K, HEAD_V)


def _lower_tri(n):
    return jnp.asarray(np.tril(np.ones((n, n), np.float32)), BF16)


def _block_tri(n, blk):
    i = np.arange(n)
    m = (i[None, :] <= i[:, None]) & (i[None, :] // blk == i[:, None] // blk)
    return jnp.asarray(m.astype(np.float32), BF16)


def _block_ones(n, blk):
    i = np.arange(n)
    return jnp.asarray((i[None, :] // blk == i[:, None] // blk).astype(np.float32), BF16)


def _group_mean_matrix():
    gsz = D_CONV // CONV_GROUPS
    i = np.arange(D_CONV)
    m = (i[None, :] // gsz == i[:, None] // gsz).astype(np.float32) / gsz
    return jnp.asarray(m, BF16)


def _layer_weights(l, norm_g, w_in, w_alpha, b_alpha, conv_w, conv_b, cn_g, cn_b, w_pw, b_pw,
                   gla_g, w_out, final_g):
    row = lambda a: a.reshape(1, -1).astype(F32)
    win = jnp.pad(w_in[l], ((0, 0), (0, D_IN_PAD - D_IN))).astype(BF16)
    wal = jnp.pad(w_alpha[l], ((0, LR_PAD - GATE_RANK), (0, 0))).astype(BF16)
    cwb = jnp.broadcast_to(conv_w[l][:, None, :], (CONV_W, SUBLANE, D_CONV)).astype(F32)
    return (row(norm_g[l]), win, wal, row(b_alpha[l]), cwb, row(conv_b[l]), row(cn_g[l]),
            row(cn_b[l]), _group_mean_matrix(), w_pw[l].astype(BF16), row(b_pw[l]),
            row(gla_g[l]), w_out[l].astype(BF16), row(final_g))


def kernel(x_prompt, x_sample, cache_conv, state_gla, norm_g, w_in, w_alpha, b_alpha, conv_w,
           conv_b, cn_g, cn_b, w_pw, b_pw, gla_g, w_out, final_g):
    depth = w_in.shape[0]
    hp, hs = x_prompt, x_sample
    conv_p, gla_p, conv_s, gla_s = [], [], [], []
    for l in range(depth):
        lw = _layer_weights(l, norm_g, w_in, w_alpha, b_alpha, conv_w, conv_b, cn_g, cn_b,
                            w_pw, b_pw, gla_g, w_out, final_g)
        final = l == depth - 1
        hp, bp, sp = _prompt_layer(hp, lw, final)
        hs, bs, ss = _sample_layer(hs, cache_conv[l], state_gla[l], lw, final)
        conv_p.append(bp); gla_p.append(sp); conv_s.append(bs); gla_s.append(ss)
    return (hp, hs, jnp.stack(conv_p), jnp.stack(gla_p), jnp.stack(conv_s), jnp.stack(gla_s))
```

```python
import functools

import jax
import jax.numpy as jnp
import numpy as np
from jax import lax
from jax.experimental import pallas as pl
from jax.experimental.pallas import tpu as pltpu

F32 = jnp.float32
BF16 = jnp.bfloat16

D_MODEL = 1024
D_CONV = 512
CONV_GROUPS = 8
CONV_W = 31
HIST = CONV_W - 1
D_V = 512
HEADS = 4
D_K = 256
HEAD_K = 64
HEAD_V = 128
GATE_RANK = 16
GATE_NORM = 16.0
EPS = 1e-6

LANE = 128
SUBLANE = 8
HIST_PAD = 32
N_LANE_GROUPS = D_CONV // LANE
LR_PAD = LANE

OFF_A, OFF_GL, OFF_ZC = 0, 512, 1024
OFF_Q, OFF_K, OFF_V, OFF_ZG, OFF_LR = 1536, 1792, 2048, 2560, 3072
D_IN = OFF_LR + GATE_RANK
D_IN_PAD = OFF_LR + LR_PAD
PROJ_COLS = OFF_LR - OFF_ZC
STAGE_COLS = 512

PROMPT_TILE = 512
PROMPT_SEG = PROMPT_TILE // SUBLANE
UHALO0 = SUBLANE
UDATA0 = UHALO0 + HIST_PAD
USEG = UDATA0 + PROMPT_SEG
CSEG = PROMPT_SEG + SUBLANE
assert USEG % 32 == 8 and CSEG % 32 == 8
GLA_CHUNK = 128
SAMPLE_SEQS = 16
CONV_UNROLL = 8

VMEM_LIMIT = 56 * 1024 * 1024


def _dot(a, b):
    return jnp.dot(a, b, preferred_element_type=F32)


def _dot_nt(a, b):
    return lax.dot_general(a, b, (((1,), (1,)), ((), ())), preferred_element_type=F32)


def _sigmoid(x):
    return 1.0 / (1.0 + jnp.exp(-x))


def _silu(x):
    return x * _sigmoid(x)


def _log_sigmoid(x):
    return jnp.minimum(x, 0.0) - jnp.log(1.0 + jnp.exp(-jnp.abs(x)))


def _rms_rows(x, g):
    ms = jnp.mean(x * x, axis=-1, keepdims=True)
    return x * lax.rsqrt(ms + EPS) * g


def _split3(g):
    hi = g.astype(BF16)
    r1 = g - hi.astype(F32)
    mid = r1.astype(BF16)
    lo = (r1 - mid.astype(F32)).astype(BF16)
    return hi, mid, lo


def _sum_matmul(m, g):
    hi, mid, lo = _split3(g)
    return _dot(m, hi) + _dot(m, mid) + _dot(m, lo)


def _conv_post(cv, zc, cb_ref, cng_ref, cnb_ref, gm_ref, wpw_ref, bpw_ref):
    cv = cv + cb_ref[...]
    gm = gm_ref[...]
    mu = _dot(cv.astype(BF16), gm)
    d = cv - mu
    var = _dot((d * d).astype(BF16), gm)
    yn = d * lax.rsqrt(var + EPS) * cng_ref[...] + cnb_ref[...]
    pw = _dot(_silu(yn).astype(BF16), wpw_ref[...]) + bpw_ref[...]
    return pw * _silu(zc)


def _gate_log(hb, win_ref, wal_ref, bal_ref):
    lr = _dot(hb, win_ref[:, OFF_LR:OFF_LR + LR_PAD])
    z = _dot(lr.astype(BF16), wal_ref[...]) + bal_ref[...]
    return _log_sigmoid(z) * (1.0 / GATE_NORM)


def _gla_post(o, zg, gg_ref):
    parts = []
    for h in range(HEADS):
        oh = o[:, h * HEAD_V:(h + 1) * HEAD_V]
        ms = jnp.mean(oh * oh, axis=-1, keepdims=True)
        parts.append(oh * lax.rsqrt(ms + EPS) * gg_ref[...])
    return jnp.concatenate(parts, axis=-1) * _silu(zg)


def _finish(x, cat, wout_ref, fg_ref, final):
    y = x + _dot(cat, wout_ref[...])
    if final:
        y = _rms_rows(y, fg_ref[...])
    return y


def _prompt_kernel(x_ref, ng_ref, win_ref, wal_ref, bal_ref, cwb_ref, cb_ref, cng_ref, cnb_ref,
                   gm_ref, wpw_ref, bpw_ref, gg_ref, wout_ref, fg_ref, ltri_ref,
                   y_ref, buf_ref, sout_ref,
                   ubuf, cvs, proj, os_, cat, sst, *, tt, final):
    t = pl.program_id(1)
    seg = tt // SUBLANE
    blocks_per_group = seg // CONV_UNROLL
    n_stage = PROJ_COLS // STAGE_COLS
    blocks_per_stage = N_LANE_GROUPS * blocks_per_group // n_stage

    last = (SUBLANE - 1) * USEG + USEG - HIST_PAD

    @pl.when(t == 0)
    def _():
        ubuf[:, UHALO0:UDATA0, :] = jnp.zeros((N_LANE_GROUPS, HIST_PAD, LANE), F32)
        sst[...] = jnp.zeros_like(sst)

    @pl.when(t > 0)
    def _():
        ubuf[:, UHALO0:UDATA0, :] = ubuf[:, last:last + HIST_PAD, :]

    x = x_ref[0]
    hb = _rms_rows(x, ng_ref[...]).astype(BF16)

    a = _dot(hb, win_ref[:, OFF_A:OFF_A + D_CONV])
    gl = _dot(hb, win_ref[:, OFF_GL:OFF_GL + D_CONV])
    u = a * _sigmoid(gl)
    for g in range(N_LANE_GROUPS):
        for j in range(SUBLANE):
            ug = u[:, g * LANE:(g + 1) * LANE]
            ubuf[g, j * USEG + UDATA0:(j + 1) * USEG, :] = ug[j * seg:(j + 1) * seg, :]
            if j > 0:
                ubuf[g, j * USEG + UHALO0:j * USEG + UDATA0, :] = ug[j * seg - HIST_PAD:j * seg, :]
    gk = _gate_log(hb, win_ref, wal_ref, bal_ref)

    base = UDATA0 - HIST

    for n in range(n_stage):
        blk0 = n * blocks_per_stage
        g = blk0 // blocks_per_group
        w = [cwb_ref[k, :, g * LANE:(g + 1) * LANE] for k in range(CONV_W)]
        for jb in range(blocks_per_stage):
            i0 = (blk0 + jb) % blocks_per_group * CONV_UNROLL
            accs = [None] * CONV_UNROLL
            for m in range(CONV_UNROLL + CONV_W - 1):
                um = ubuf[g, pl.ds(i0 + (base + m), SUBLANE, stride=USEG), :]
                for ii in range(CONV_UNROLL):
                    k = m - ii
                    if 0 <= k < CONV_W:
                        accs[ii] = um * w[k] if accs[ii] is None else accs[ii] + um * w[k]
            for ii in range(CONV_UNROLL):
                cvs[g, pl.ds(i0 + ii, SUBLANE, stride=CSEG), :] = accs[ii]
        col0 = n * STAGE_COLS
        proj[:, col0:col0 + STAGE_COLS] = _dot(
            hb, win_ref[:, OFF_ZC + col0:OFF_ZC + col0 + STAGE_COLS])

    cv = jnp.concatenate(
        [jnp.concatenate([cvs[g, j * CSEG:j * CSEG + seg, :] for j in range(SUBLANE)], axis=0)
         for g in range(N_LANE_GROUPS)], axis=-1)
    zc = proj[:, OFF_ZC - OFF_ZC:OFF_Q - OFF_ZC]
    cvo = _conv_post(cv, zc, cb_ref, cng_ref, cnb_ref, gm_ref, wpw_ref, bpw_ref)
    cat[:, 0:D_CONV] = cvo.astype(BF16)

    c = GLA_CHUNK
    half = c // 2
    row = lax.broadcasted_iota(jnp.int32, (c, c), 0)
    col = lax.broadcasted_iota(jnp.int32, (c, c), 1)
    causal = col <= row
    head_lo = col < HEAD_K
    ltri = ltri_ref[...]

    s_old = sst[...]
    for i in range(tt // c):
        rows = slice(i * c, (i + 1) * c)
        gcum = _sum_matmul(ltri, gk[rows, :])
        q = proj[rows, OFF_Q - OFF_ZC:OFF_K - OFF_ZC] * (HEAD_K ** -0.5)
        k = proj[rows, OFF_K - OFF_ZC:OFF_V - OFF_ZC]
        v = proj[rows, OFF_V - OFF_ZC:OFF_ZG - OFF_ZC].astype(BF16)
        mid = gcum[half - 1:half, :]
        xq = q * jnp.exp(gcum - mid)
        yk = (k * jnp.exp(mid - gcum)).astype(BF16)
        qg = q * jnp.exp(gcum)
        gt = gcum.T
        glast = gt[:, c - 1:c]
        kdt = (k.T * jnp.exp(glast - gt)).astype(BF16)
        dcol = jnp.exp(glast)
        s_bf = s_old.astype(BF16)
        outs = []
        s_new = []
        for h in range(HEADS):
            j, hi = divmod(h, 2)
            pair = slice(j * LANE, (j + 1) * LANE)
            hm = head_lo if hi == 0 else jnp.logical_not(head_lo)
            xm = jnp.where(hm, xq[:, pair], 0.0).astype(BF16)
            qm = jnp.where(hm, qg[:, pair], 0.0).astype(BF16)
            att = jnp.where(causal, _dot_nt(xm, yk[:, pair]), 0.0).astype(BF16)
            vh = v[:, h * HEAD_V:(h + 1) * HEAD_V]
            outs.append(_dot(att, vh) + _dot(qm, s_bf[pair, :]))
            hr = slice(h * HEAD_K, (h + 1) * HEAD_K)
            s_new.append(dcol[hr, :] * s_old[hr, :] + _dot(kdt[hr, :], vh))
        s_old = jnp.concatenate(s_new, axis=0)
        os_[rows, :] = jnp.concatenate(outs, axis=-1)
    sst[...] = s_old

    zg = proj[:, OFF_ZG - OFF_ZC:OFF_LR - OFF_ZC]
    cat[:, D_CONV:] = _gla_post(os_[...], zg, gg_ref).astype(BF16)
    y_ref[0] = _finish(x, cat[...], wout_ref, fg_ref, final)

    @pl.when(t == pl.num_programs(1) - 1)
    def _():
        buf_ref[0] = jnp.concatenate(
            [ubuf[g, last + HIST_PAD - HIST:last + HIST_PAD, :] for g in range(N_LANE_GROUPS)],
            axis=-1)
        sout_ref[0] = sst[...]


def _const_spec(shape):
    return pl.BlockSpec(shape, lambda *_: (0,) * len(shape), pipeline_mode=pl.Buffered(1))


def _prompt_layer(x, lw, final):
    b, t, _ = x.shape
    tt = PROMPT_TILE
    assert t % tt == 0 and tt % GLA_CHUNK == 0 and tt >= HIST_PAD
    blocks_per_group = tt // SUBLANE // CONV_UNROLL
    n_stage = PROJ_COLS // STAGE_COLS
    assert tt % (SUBLANE * CONV_UNROLL) == 0 and PROJ_COLS % STAGE_COLS == 0
    assert (N_LANE_GROUPS * blocks_per_group) % n_stage == 0
    assert blocks_per_group % (N_LANE_GROUPS * blocks_per_group // n_stage) == 0
    consts = lw + (_lower_tri(GLA_CHUNK),)
    in_specs = [pl.BlockSpec((1, tt, D_MODEL), lambda i, j: (i, j, 0))]
    in_specs += [_const_spec(c.shape) for c in consts]
    out_shape = (jax.ShapeDtypeStruct((b, t, D_MODEL), F32),
                 jax.ShapeDtypeStruct((b, HIST, D_CONV), F32),
                 jax.ShapeDtypeStruct((b, D_K, HEAD_V), F32))
    out_specs = (pl.BlockSpec((1, tt, D_MODEL), lambda i, j: (i, j, 0)),
                 pl.BlockSpec((1, HIST, D_CONV), lambda i, j: (i, 0, 0)),
                 pl.BlockSpec((1, D_K, HEAD_V), lambda i, j: (i, 0, 0)))
    scratch = [pltpu.VMEM((N_LANE_GROUPS, SUBLANE * USEG, LANE), F32),
               pltpu.VMEM((N_LANE_GROUPS, SUBLANE * CSEG, LANE), F32),
               pltpu.VMEM((tt, PROJ_COLS), F32),
               pltpu.VMEM((tt, D_V), F32),
               pltpu.VMEM((tt, D_MODEL), BF16),
               pltpu.VMEM((D_K, HEAD_V), F32)]
    y, buf, s = pl.pallas_call(
        functools.partial(_prompt_kernel, tt=tt, final=final),
        grid=(b, t // tt),
        in_specs=in_specs, out_specs=out_specs, out_shape=out_shape,
        scratch_shapes=scratch,
        compiler_params=pltpu.CompilerParams(
            dimension_semantics=("arbitrary", "arbitrary"), vmem_limit_bytes=VMEM_LIMIT),
        name="prompt_layer",
    )(x, *consts)
    return y, buf, s.reshape(b, HEADS, HEAD_K, HEAD_V)


def _sample_kernel(x_ref, cache_ref, s0_ref, ng_ref, win_ref, wal_ref, bal_ref, cwb_ref, cb_ref,
                   cng_ref, cnb_ref, gm_ref, wpw_ref, bpw_ref, gg_ref, wout_ref, fg_ref,
                   ltri_ref, ball_ref,
                   y_ref, buf_ref, sout_ref, full, *, nb, ts, final):
    r = nb * ts
    x = x_ref[...].reshape(r, D_MODEL)
    hb = _rms_rows(x, ng_ref[...]).astype(BF16)

    a = _dot(hb, win_ref[:, OFF_A:OFF_A + D_CONV])
    gl = _dot(hb, win_ref[:, OFF_GL:OFF_GL + D_CONV])
    full[:, 0:HIST, :] = cache_ref[...]
    full[:, HIST:HIST + ts, :] = (a * _sigmoid(gl)).reshape(nb, ts, D_CONV)
    acc = full[:, 0:ts, :] * cwb_ref[0]
    for k in range(1, CONV_W):
        acc = acc + full[:, k:k + ts, :] * cwb_ref[k]
    buf_ref[...] = full[:, ts:ts + HIST, :]
    zc = _dot(hb, win_ref[:, OFF_ZC:OFF_ZC + D_CONV])
    cvo = _conv_post(acc.reshape(r, D_CONV), zc, cb_ref, cng_ref, cnb_ref, gm_ref,
                     wpw_ref, bpw_ref)

    q = _dot(hb, win_ref[:, OFF_Q:OFF_Q + D_K]) * (HEAD_K ** -0.5)
    k = _dot(hb, win_ref[:, OFF_K:OFF_K + D_K])
    v = _dot(hb, win_ref[:, OFF_V:OFF_V + D_V]).astype(BF16)
    g = _gate_log(hb, win_ref, wal_ref, bal_ref)
    gcum = _sum_matmul(ltri_ref[...], g)
    gtot = _sum_matmul(ball_ref[...], g)
    xq = q * jnp.exp(gcum)
    yk = (k * jnp.exp(-gcum)).astype(BF16)
    kdt = (k * jnp.exp(gtot - gcum)).T.astype(BF16)
    gt = g.T

    row = lax.broadcasted_iota(jnp.int32, (r, r), 0)
    col = lax.broadcasted_iota(jnp.int32, (r, r), 1)
    causal = jnp.logical_and(col <= row, (col // ts) == (row // ts))
    head_lo = lax.broadcasted_iota(jnp.int32, (r, LANE), 1) < HEAD_K
    seq_sel = (lax.broadcasted_iota(jnp.int32, (nb, 1, r), 2) // ts
               == lax.broadcasted_iota(jnp.int32, (nb, 1, r), 0))
    blk = (lax.broadcasted_iota(jnp.int32, (r, nb * LANE), 1) // LANE
           == lax.broadcasted_iota(jnp.int32, (r, nb * LANE), 0) // ts)

    s0 = s0_ref[...]
    s0_bf = s0.astype(BF16)
    gsum = jnp.sum(jnp.where(seq_sel, gt[None, :, :], 0.0), axis=-1, keepdims=True)
    s_new = jnp.exp(gsum) * s0

    outs = []
    kvs = []
    for h in range(HEADS):
        j, hi = divmod(h, 2)
        pair = slice(j * LANE, (j + 1) * LANE)
        hm = head_lo if hi == 0 else jnp.logical_not(head_lo)
        xm = jnp.where(hm, xq[:, pair], 0.0).astype(BF16)
        att = jnp.where(causal, _dot_nt(xm, yk[:, pair]), 0.0).astype(BF16)
        vh = v[:, h * HEAD_V:(h + 1) * HEAD_V]
        xblk = jnp.where(blk, jnp.concatenate([xm] * nb, axis=-1), jnp.zeros((), BF16))
        s_pair = s0_bf[:, pair, :].reshape(nb * LANE, HEAD_V)
        outs.append(_dot(att, vh) + _dot(xblk, s_pair))
        kh = kdt[h * HEAD_K:(h + 1) * HEAD_K, :]
        kblk = jnp.where(seq_sel, kh[None, :, :], jnp.zeros((), BF16)).reshape(nb * HEAD_K, r)
        kvs.append(_dot(kblk, vh).reshape(nb, HEAD_K, HEAD_V))
    sout_ref[...] = s_new + jnp.concatenate(kvs, axis=1)

    o = jnp.concatenate(outs, axis=-1)
    zg = _dot(hb, win_ref[:, OFF_ZG:OFF_ZG + D_V])
    cat = jnp.concatenate([cvo.astype(BF16), _gla_post(o, zg, gg_ref).astype(BF16)], axis=-1)
    y_ref[...] = _finish(x, cat, wout_ref, fg_ref, final).reshape(nb, ts, D_MODEL)


def _sample_layer(x, cache, s0, lw, final):
    b, ts, _ = x.shape
    nb = SAMPLE_SEQS
    assert b % nb == 0 and ts == SUBLANE and nb * ts == LANE
    r = nb * ts
    consts = lw + (_block_tri(r, ts), _block_ones(r, ts))
    s0 = s0.reshape(b, D_K, HEAD_V)
    in_specs = [pl.BlockSpec((nb, ts, D_MODEL), lambda i: (i, 0, 0)),
                pl.BlockSpec((nb, HIST, D_CONV), lambda i: (i, 0, 0)),
                pl.BlockSpec((nb, D_K, HEAD_V), lambda i: (i, 0, 0))]
    in_specs += [_const_spec(c.shape) for c in consts]
    out_shape = (jax.ShapeDtypeStruct((b, ts, D_MODEL), F32),
                 jax.ShapeDtypeStruct((b, HIST, D_CONV), F32),
                 jax.ShapeDtypeStruct((b, D_K, HEAD_V), F32))
    out_specs = (pl.BlockSpec((nb, ts, D_MODEL), lambda i: (i, 0, 0)),
                 pl.BlockSpec((nb, HIST, D_CONV), lambda i: (i, 0, 0)),
                 pl.BlockSpec((nb, D_K, HEAD_V), lambda i: (i, 0, 0)))
    y, buf, s = pl.pallas_call(
        functools.partial(_sample_kernel, nb=nb, ts=ts, final=final),
        grid=(b // nb,),
        in_specs=in_specs, out_specs=out_specs, out_shape=out_shape,
        scratch_shapes=[pltpu.VMEM((nb, HIST + ts + 2, D_CONV), F32)],
        compiler_params=pltpu.CompilerParams(
            dimension_semantics=("arbitrary",), vmem_limit_bytes=VMEM_LIMIT),
        name="sample_layer",
    )(x, cache, s0, *consts)
    return y, buf, s.reshape(b, HEADS, HEAD_K, HEAD_V)


def _lower_tri(n):
    return jnp.asarray(np.tril(np.ones((n, n), np.float32)), BF16)


def _block_tri(n, blk):
    i = np.arange(n)
    m = (i[None, :] <= i[:, None]) & (i[None, :] // blk == i[:, None] // blk)
    return jnp.asarray(m.astype(np.float32), BF16)


def _block_ones(n, blk):
    i = np.arange(n)
    return jnp.asarray((i[None, :] // blk == i[:, None] // blk).astype(np.float32), BF16)


def _group_mean_matrix():
    gsz = D_CONV // CONV_GROUPS
    i = np.arange(D_CONV)
    m = (i[None, :] // gsz == i[:, None] // gsz).astype(np.float32) / gsz
    return jnp.asarray(m, BF16)


def _layer_weights(l, norm_g, w_in, w_alpha, b_alpha, conv_w, conv_b, cn_g, cn_b, w_pw, b_pw,
                   gla_g, w_out, final_g):
    row = lambda a: a.reshape(1, -1).astype(F32)
    win = jnp.pad(w_in[l], ((0, 0), (0, D_IN_PAD - D_IN))).astype(BF16)
    wal = jnp.pad(w_alpha[l], ((0, LR_PAD - GATE_RANK), (0, 0))).astype(BF16)
    cwb = jnp.broadcast_to(conv_w[l][:, None, :], (CONV_W, SUBLANE, D_CONV)).astype(F32)
    return (row(norm_g[l]), win, wal, row(b_alpha[l]), cwb, row(conv_b[l]), row(cn_g[l]),
            row(cn_b[l]), _group_mean_matrix(), w_pw[l].astype(BF16), row(b_pw[l]),
            row(gla_g[l]), w_out[l].astype(BF16), row(final_g))


def kernel(x_prompt, x_sample, cache_conv, state_gla, norm_g, w_in, w_alpha, b_alpha, conv_w,
           conv_b, cn_g, cn_b, w_pw, b_pw, gla_g, w_out, final_g):
    depth = w_in.shape[0]
    hp, hs = x_prompt, x_sample
    conv_p, gla_p, conv_s, gla_s = [], [], [], []
    for l in range(depth):
        lw = _layer_weights(l, norm_g, w_in, w_alpha, b_alpha, conv_w, conv_b, cn_g, cn_b,
                            w_pw, b_pw, gla_g, w_out, final_g)
        final = l == depth - 1
        hp, bp, sp = _prompt_layer(hp, lw, final)
        hs, bs, ss = _sample_layer(hs, cache_conv[l], state_gla[l], lw, final)
        conv_p.append(bp); gla_p.append(sp); conv_s.append(bs); gla_s.append(ss)
    return (hp, hs, jnp.stack(conv_p), jnp.stack(gla_p), jnp.stack(conv_s), jnp.stack(gla_s))
```

```python
import functools

import jax
import jax.numpy as jnp
import numpy as np
from jax import lax
from jax.experimental import pallas as pl
from jax.experimental.pallas import tpu as pltpu

F32 = jnp.float32
BF16 = jnp.bfloat16

D_MODEL = 1024
D_CONV = 512
CONV_GROUPS = 8
CONV_W = 31
HIST = CONV_W - 1
D_V = 512
HEADS = 4
D_K = 256
HEAD_K = 64
HEAD_V = 128
GATE_RANK = 16
GATE_NORM = 16.0
EPS = 1e-6

LANE = 128
SUBLANE = 8
HIST_PAD = 32
N_LANE_GROUPS = D_CONV // LANE
LR_PAD = LANE

OFF_A, OFF_GL, OFF_ZC = 0, 512, 1024
OFF_Q, OFF_K, OFF_V, OFF_ZG, OFF_LR = 1536, 1792, 2048, 2560, 3072
D_IN = OFF_LR + GATE_RANK
D_IN_PAD = OFF_LR + LR_PAD
PROJ_COLS = OFF_LR - OFF_ZC
STAGE_COLS = 512

PROMPT_TILE = 512
PROMPT_SEG = PROMPT_TILE // SUBLANE
UHALO0 = SUBLANE
UDATA0 = UHALO0 + HIST_PAD
USEG = UDATA0 + PROMPT_SEG
CSEG = PROMPT_SEG + SUBLANE
assert USEG % 32 == 8 and CSEG % 32 == 8
GLA_CHUNK = 128
SAMPLE_SEQS = 16
CAST_ROWS = 64
CONV_UNROLL = 8

VMEM_LIMIT = 56 * 1024 * 1024


def _dot(a, b):
    return jnp.dot(a, b, preferred_element_type=F32)


def _dot_nt(a, b):
    return lax.dot_general(a, b, (((1,), (1,)), ((), ())), preferred_element_type=F32)


def _sigmoid(x):
    return 1.0 / (1.0 + jnp.exp(-x))


def _silu(x):
    return x * _sigmoid(x)


def _log_sigmoid(x):
    return jnp.minimum(x, 0.0) - jnp.log(1.0 + jnp.exp(-jnp.abs(x)))


def _rms_rows(x, g):
    ms = jnp.mean(x * x, axis=-1, keepdims=True)
    return x * lax.rsqrt(ms + EPS) * g


def _split3(g):
    hi = g.astype(BF16)
    r1 = g - hi.astype(F32)
    mid = r1.astype(BF16)
    lo = (r1 - mid.astype(F32)).astype(BF16)
    return hi, mid, lo


def _sum_matmul(m, g):
    hi, mid, lo = _split3(g)
    return _dot(m, hi) + _dot(m, mid) + _dot(m, lo)


def _conv_post(cv, zc, cb_ref, cng_ref, cnb_ref, gm_ref, wpw_ref, bpw_ref):
    cv = cv + cb_ref[...]
    gm = gm_ref[...]
    mu = _dot(cv.astype(BF16), gm)
    d = cv - mu
    var = _dot((d * d).astype(BF16), gm)
    yn = d * lax.rsqrt(var + EPS) * cng_ref[...] + cnb_ref[...]
    pw = _dot(_silu(yn).astype(BF16), wpw_ref[...]) + bpw_ref[...]
    return pw * _silu(zc)


def _gate_log(hb, wlr_ref, wal_ref, bal_ref):
    lr = _dot(hb, wlr_ref[...])
    z = _dot(lr.astype(BF16), wal_ref[...]) + bal_ref[...]
    return _log_sigmoid(z) * (1.0 / GATE_NORM)


def _cast_rows(src_ref, dst_ref):
    k, n = dst_ref.shape

    def body(i, carry):
        rows = pl.ds(pl.multiple_of(i * CAST_ROWS, CAST_ROWS), CAST_ROWS)
        dst_ref[rows, :] = src_ref[rows, 0:n].astype(BF16)
        return carry

    lax.fori_loop(0, k // CAST_ROWS, body, 0)


def _cast_weights(win_ref, wpw_ref, wout_ref, winb, wpwb, woutb):
    _cast_rows(win_ref, winb)
    _cast_rows(wpw_ref, wpwb)
    _cast_rows(wout_ref, woutb)


def _gla_post(o, zg, gg_ref):
    parts = []
    for h in range(HEADS):
        oh = o[:, h * HEAD_V:(h + 1) * HEAD_V]
        ms = jnp.mean(oh * oh, axis=-1, keepdims=True)
        parts.append(oh * lax.rsqrt(ms + EPS) * gg_ref[...])
    return jnp.concatenate(parts, axis=-1) * _silu(zg)


def _finish(x, cat, wout_ref, fg_ref, final):
    y = x + _dot(cat, wout_ref[...])
    if final:
        y = _rms_rows(y, fg_ref[...])
    return y


def _prompt_kernel(x_ref, ng_ref, win_ref, wlr_ref, wal_ref, bal_ref, cwb_ref, cb_ref, cng_ref,
                   cnb_ref, gm_ref, wpw_ref, bpw_ref, gg_ref, wout_ref, fg_ref, ltri_ref, *rest,
                   tt, final):
    (y_ref, buf_ref, sout_ref, ubuf, cvs, proj, os_, cat, sst, winb, wpwb, woutb) = rest[-12:]
    t = pl.program_id(1)

    @pl.when(jnp.logical_and(pl.program_id(0) == 0, t == 0))
    def _():
        _cast_weights(win_ref, wpw_ref, wout_ref, winb, wpwb, woutb)

    seg = tt // SUBLANE
    blocks_per_group = seg // CONV_UNROLL
    n_stage = PROJ_COLS // STAGE_COLS
    blocks_per_stage = N_LANE_GROUPS * blocks_per_group // n_stage

    last = (SUBLANE - 1) * USEG + USEG - HIST_PAD

    @pl.when(t == 0)
    def _():
        ubuf[:, UHALO0:UDATA0, :] = jnp.zeros((N_LANE_GROUPS, HIST_PAD, LANE), F32)
        sst[...] = jnp.zeros_like(sst)

    @pl.when(t > 0)
    def _():
        ubuf[:, UHALO0:UDATA0, :] = ubuf[:, last:last + HIST_PAD, :]

    x = x_ref[0]
    hb = _rms_rows(x, ng_ref[...]).astype(BF16)

    a = _dot(hb, winb[:, OFF_A:OFF_A + D_CONV])
    gl = _dot(hb, winb[:, OFF_GL:OFF_GL + D_CONV])
    u = a * _sigmoid(gl)
    for g in range(N_LANE_GROUPS):
        for j in range(SUBLANE):
            ug = u[:, g * LANE:(g + 1) * LANE]
            ubuf[g, j * USEG + UDATA0:(j + 1) * USEG, :] = ug[j * seg:(j + 1) * seg, :]
            if j > 0:
                ubuf[g, j * USEG + UHALO0:j * USEG + UDATA0, :] = ug[j * seg - HIST_PAD:j * seg, :]
    gk = _gate_log(hb, wlr_ref, wal_ref, bal_ref)

    base = UDATA0 - HIST

    for n in range(n_stage):
        blk0 = n * blocks_per_stage
        g = blk0 // blocks_per_group
        w = [cwb_ref[k, :, g * LANE:(g + 1) * LANE] for k in range(CONV_W)]
        for jb in range(blocks_per_stage):
            i0 = (blk0 + jb) % blocks_per_group * CONV_UNROLL
            accs = [None] * CONV_UNROLL
            for m in range(CONV_UNROLL + CONV_W - 1):
                um = ubuf[g, pl.ds(i0 + (base + m), SUBLANE, stride=USEG), :]
                for ii in range(CONV_UNROLL):
                    k = m - ii
                    if 0 <= k < CONV_W:
                        accs[ii] = um * w[k] if accs[ii] is None else accs[ii] + um * w[k]
            for ii in range(CONV_UNROLL):
                cvs[g, pl.ds(i0 + ii, SUBLANE, stride=CSEG), :] = accs[ii]
        col0 = n * STAGE_COLS
        proj[:, col0:col0 + STAGE_COLS] = _dot(
            hb, winb[:, OFF_ZC + col0:OFF_ZC + col0 + STAGE_COLS])

    cv = jnp.concatenate(
        [jnp.concatenate([cvs[g, j * CSEG:j * CSEG + seg, :] for j in range(SUBLANE)], axis=0)
         for g in range(N_LANE_GROUPS)], axis=-1)
    zc = proj[:, OFF_ZC - OFF_ZC:OFF_Q - OFF_ZC]
    cvo = _conv_post(cv, zc, cb_ref, cng_ref, cnb_ref, gm_ref, wpwb, bpw_ref)
    cat[:, 0:D_CONV] = cvo.astype(BF16)

    c = GLA_CHUNK
    half = c // 2
    row = lax.broadcasted_iota(jnp.int32, (c, c), 0)
    col = lax.broadcasted_iota(jnp.int32, (c, c), 1)
    causal = col <= row
    head_lo = col < HEAD_K
    ltri = ltri_ref[...]

    s_old = sst[...]
    for i in range(tt // c):
        rows = slice(i * c, (i + 1) * c)
        gcum = _sum_matmul(ltri, gk[rows, :])
        q = proj[rows, OFF_Q - OFF_ZC:OFF_K - OFF_ZC] * (HEAD_K ** -0.5)
        k = proj[rows, OFF_K - OFF_ZC:OFF_V - OFF_ZC]
        v = proj[rows, OFF_V - OFF_ZC:OFF_ZG - OFF_ZC].astype(BF16)
        mid = gcum[half - 1:half, :]
        xq = q * jnp.exp(gcum - mid)
        yk = (k * jnp.exp(mid - gcum)).astype(BF16)
        qg = q * jnp.exp(gcum)
        gt = gcum.T
        glast = gt[:, c - 1:c]
        kdt = (k.T * jnp.exp(glast - gt)).astype(BF16)
        dcol = jnp.exp(glast)
        s_bf = s_old.astype(BF16)
        outs = []
        s_new = []
        for h in range(HEADS):
            j, hi = divmod(h, 2)
            pair = slice(j * LANE, (j + 1) * LANE)
            hm = head_lo if hi == 0 else jnp.logical_not(head_lo)
            xm = jnp.where(hm, xq[:, pair], 0.0).astype(BF16)
            qm = jnp.where(hm, qg[:, pair], 0.0).astype(BF16)
            att = jnp.where(causal, _dot_nt(xm, yk[:, pair]), 0.0).astype(BF16)
            vh = v[:, h * HEAD_V:(h + 1) * HEAD_V]
            outs.append(_dot(att, vh) + _dot(qm, s_bf[pair, :]))
            hr = slice(h * HEAD_K, (h + 1) * HEAD_K)
            s_new.append(dcol[hr, :] * s_old[hr, :] + _dot(kdt[hr, :], vh))
        s_old = jnp.concatenate(s_new, axis=0)
        os_[rows, :] = jnp.concatenate(outs, axis=-1)
    sst[...] = s_old

    zg = proj[:, OFF_ZG - OFF_ZC:OFF_LR - OFF_ZC]
    cat[:, D_CONV:] = _gla_post(os_[...], zg, gg_ref).astype(BF16)
    y_ref[0] = _finish(x, cat[...], woutb, fg_ref, final)

    @pl.when(t == pl.num_programs(1) - 1)
    def _():
        buf_ref[0] = jnp.concatenate(
            [ubuf[g, last + HIST_PAD - HIST:last + HIST_PAD, :] for g in range(N_LANE_GROUPS)],
            axis=-1)
        sout_ref[0] = sst[...]


def _const_spec(a):
    return pl.BlockSpec(a.shape, lambda *_: (0,) * a.ndim, pipeline_mode=pl.Buffered(1))


def _layer_spec(a, l):
    return pl.BlockSpec((None,) + a.shape[1:], lambda *_: (l,) + (0,) * (a.ndim - 1),
                        pipeline_mode=pl.Buffered(1))


def _operands(l, lw, masks, prev):
    ops = list(lw) + list(masks) + list(prev)
    specs = [_layer_spec(a, l) if i in LAYER_STACKED else _const_spec(a) for i, a in enumerate(lw)]
    specs += [_const_spec(m) for m in masks]
    specs += [pl.BlockSpec(memory_space=pl.ANY) for _ in prev]
    return ops, specs


def _weight_scratch():
    return [pltpu.VMEM((D_MODEL, OFF_LR), BF16),
            pltpu.VMEM((D_CONV, D_CONV), BF16),
            pltpu.VMEM((D_MODEL, D_MODEL), BF16)]


def _prompt_layer(l, depth, x, lw, prev, final):
    b, t, _ = x.shape
    tt = PROMPT_TILE
    assert t % tt == 0 and tt % GLA_CHUNK == 0 and tt >= HIST_PAD
    blocks_per_group = tt // SUBLANE // CONV_UNROLL
    n_stage = PROJ_COLS // STAGE_COLS
    assert tt % (SUBLANE * CONV_UNROLL) == 0 and PROJ_COLS % STAGE_COLS == 0
    assert (N_LANE_GROUPS * blocks_per_group) % n_stage == 0
    assert blocks_per_group % (N_LANE_GROUPS * blocks_per_group // n_stage) == 0
    ops, specs = _operands(l, lw, (_lower_tri(GLA_CHUNK),), prev)
    in_specs = [pl.BlockSpec((1, tt, D_MODEL), lambda i, j: (i, j, 0))] + specs
    out_shape = (jax.ShapeDtypeStruct((b, t, D_MODEL), F32),
                 jax.ShapeDtypeStruct((depth, b, HIST, D_CONV), F32),
                 jax.ShapeDtypeStruct((depth, b, D_K, HEAD_V), F32))
    out_specs = (pl.BlockSpec((1, tt, D_MODEL), lambda i, j: (i, j, 0)),
                 pl.BlockSpec((None, 1, HIST, D_CONV), lambda i, j: (l, i, 0, 0)),
                 pl.BlockSpec((None, 1, D_K, HEAD_V), lambda i, j: (l, i, 0, 0)))
    n_in = 1 + len(ops)
    aliases = {n_in - len(prev) + k: 1 + k for k in range(len(prev))}
    scratch = [pltpu.VMEM((N_LANE_GROUPS, SUBLANE * USEG, LANE), F32),
               pltpu.VMEM((N_LANE_GROUPS, SUBLANE * CSEG, LANE), F32),
               pltpu.VMEM((tt, PROJ_COLS), F32),
               pltpu.VMEM((tt, D_V), F32),
               pltpu.VMEM((tt, D_MODEL), BF16),
               pltpu.VMEM((D_K, HEAD_V), F32)]
    return pl.pallas_call(
        functools.partial(_prompt_kernel, tt=tt, final=final),
        grid=(b, t // tt),
        in_specs=in_specs, out_specs=out_specs, out_shape=out_shape,
        scratch_shapes=scratch + _weight_scratch(),
        input_output_aliases=aliases,
        compiler_params=pltpu.CompilerParams(
            dimension_semantics=("arbitrary", "arbitrary"), vmem_limit_bytes=VMEM_LIMIT),
        name="prompt_layer",
    )(x, *ops)


def _sample_kernel(x_ref, cache_ref, s0_ref, ng_ref, win_ref, wlr_ref, wal_ref, bal_ref, cwb_ref,
                   cb_ref, cng_ref, cnb_ref, gm_ref, wpw_ref, bpw_ref, gg_ref, wout_ref, fg_ref,
                   ltri_ref, ball_ref, *rest, nb, ts, final):
    (y_ref, buf_ref, sout_ref, full, winb, wpwb, woutb) = rest[-7:]

    @pl.when(pl.program_id(0) == 0)
    def _():
        _cast_weights(win_ref, wpw_ref, wout_ref, winb, wpwb, woutb)

    r = nb * ts
    x = x_ref[...].reshape(r, D_MODEL)
    hb = _rms_rows(x, ng_ref[...]).astype(BF16)

    a = _dot(hb, winb[:, OFF_A:OFF_A + D_CONV])
    gl = _dot(hb, winb[:, OFF_GL:OFF_GL + D_CONV])
    full[:, 0:HIST, :] = cache_ref[...]
    full[:, HIST:HIST + ts, :] = (a * _sigmoid(gl)).reshape(nb, ts, D_CONV)
    acc = full[:, 0:ts, :] * cwb_ref[0]
    for k in range(1, CONV_W):
        acc = acc + full[:, k:k + ts, :] * cwb_ref[k]
    buf_ref[...] = full[:, ts:ts + HIST, :]
    zc = _dot(hb, winb[:, OFF_ZC:OFF_ZC + D_CONV])
    cvo = _conv_post(acc.reshape(r, D_CONV), zc, cb_ref, cng_ref, cnb_ref, gm_ref,
                     wpwb, bpw_ref)

    q = _dot(hb, winb[:, OFF_Q:OFF_Q + D_K]) * (HEAD_K ** -0.5)
    k = _dot(hb, winb[:, OFF_K:OFF_K + D_K])
    v = _dot(hb, winb[:, OFF_V:OFF_V + D_V]).astype(BF16)
    g = _gate_log(hb, wlr_ref, wal_ref, bal_ref)
    gcum = _sum_matmul(ltri_ref[...], g)
    gtot = _sum_matmul(ball_ref[...], g)
    xq = q * jnp.exp(gcum)
    yk = (k * jnp.exp(-gcum)).astype(BF16)
    kdt = (k * jnp.exp(gtot - gcum)).T.astype(BF16)
    gt = g.T

    row = lax.broadcasted_iota(jnp.int32, (r, r), 0)
    col = lax.broadcasted_iota(jnp.int32, (r, r), 1)
    causal = jnp.logical_and(col <= row, (col // ts) == (row // ts))
    head_lo = lax.broadcasted_iota(jnp.int32, (r, LANE), 1) < HEAD_K
    seq_sel = (lax.broadcasted_iota(jnp.int32, (nb, 1, r), 2) // ts
               == lax.broadcasted_iota(jnp.int32, (nb, 1, r), 0))
    blk = (lax.broadcasted_iota(jnp.int32, (r, nb * LANE), 1) // LANE
           == lax.broadcasted_iota(jnp.int32, (r, nb * LANE), 0) // ts)

    s0 = s0_ref[...]
    s0_bf = s0.astype(BF16)
    gsum = jnp.sum(jnp.where(seq_sel, gt[None, :, :], 0.0), axis=-1, keepdims=True)
    s_new = jnp.exp(gsum) * s0

    outs = []
    kvs = []
    for h in range(HEADS):
        j, hi = divmod(h, 2)
        pair = slice(j * LANE, (j + 1) * LANE)
        hm = head_lo if hi == 0 else jnp.logical_not(head_lo)
        xm = jnp.where(hm, xq[:, pair], 0.0).astype(BF16)
        att = jnp.where(causal, _dot_nt(xm, yk[:, pair]), 0.0).astype(BF16)
        vh = v[:, h * HEAD_V:(h + 1) * HEAD_V]
        xblk = jnp.where(blk, jnp.concatenate([xm] * nb, axis=-1), jnp.zeros((), BF16))
        s_pair = s0_bf[:, pair, :].reshape(nb * LANE, HEAD_V)
        outs.append(_dot(att, vh) + _dot(xblk, s_pair))
        kh = kdt[h * HEAD_K:(h + 1) * HEAD_K, :]
        kblk = jnp.where(seq_sel, kh[None, :, :], jnp.zeros((), BF16)).reshape(nb * HEAD_K, r)
        kvs.append(_dot(kblk, vh).reshape(nb, HEAD_K, HEAD_V))
    sout_ref[...] = s_new + jnp.concatenate(kvs, axis=1)

    o = jnp.concatenate(outs, axis=-1)
    zg = _dot(hb, winb[:, OFF_ZG:OFF_ZG + D_V])
    cat = jnp.concatenate([cvo.astype(BF16), _gla_post(o, zg, gg_ref).astype(BF16)], axis=-1)
    y_ref[...] = _finish(x, cat, woutb, fg_ref, final).reshape(nb, ts, D_MODEL)


def _sample_layer(l, x, cache, s0, lw, prev, final):
    b, ts, _ = x.shape
    depth = cache.shape[0]
    nb = SAMPLE_SEQS
    assert b % nb == 0 and ts == SUBLANE and nb * ts == LANE
    r = nb * ts
    ops, specs = _operands(l, lw, (_block_tri(r, ts), _block_ones(r, ts)), prev)
    state_specs = [pl.BlockSpec((None, nb, HIST, D_CONV), lambda i: (l, i, 0, 0)),
                   pl.BlockSpec((None, nb, D_K, HEAD_V), lambda i: (l, i, 0, 0))]
    in_specs = [pl.BlockSpec((nb, ts, D_MODEL), lambda i: (i, 0, 0))] + state_specs + specs
    out_shape = (jax.ShapeDtypeStruct((b, ts, D_MODEL), F32),
                 jax.ShapeDtypeStruct(cache.shape, F32),
                 jax.ShapeDtypeStruct(s0.shape, F32))
    out_specs = [pl.BlockSpec((nb, ts, D_MODEL), lambda i: (i, 0, 0))] + state_specs
    n_in = 3 + len(ops)
    aliases = {n_in - len(prev) + k: 1 + k for k in range(len(prev))}
    return pl.pallas_call(
        functools.partial(_sample_kernel, nb=nb, ts=ts, final=final),
        grid=(b // nb,),
        in_specs=in_specs, out_specs=out_specs, out_shape=out_shape,
        scratch_shapes=[pltpu.VMEM((nb, HIST + ts + 2, D_CONV), F32)] + _weight_scratch(),
        input_output_aliases=aliases,
        compiler_params=pltpu.CompilerParams(
            dimension_semantics=("arbitrary",), vmem_limit_bytes=VMEM_LIMIT),
        name="sample_layer",
    )(x, cache, s0, *ops)


def _lower_tri(n):
    return jnp.asarray(np.tril(np.ones((n, n), np.float32)), BF16)


def _block_tri(n, blk):
    i = np.arange(n)
    m = (i[None, :] <= i[:, None]) & (i[None, :] // blk == i[:, None] // blk)
    return jnp.asarray(m.astype(np.float32), BF16)


def _block_ones(n, blk):
    i = np.arange(n)
    return jnp.asarray((i[None, :] // blk == i[:, None] // blk).astype(np.float32), BF16)


def _group_mean_matrix():
    gsz = D_CONV // CONV_GROUPS
    i = np.arange(D_CONV)
    m = (i[None, :] // gsz == i[:, None] // gsz).astype(np.float32) / gsz
    return jnp.asarray(m, BF16)


def _layer_weights(l, norm_g, w_in, w_alpha, b_alpha, conv_w, conv_b, cn_g, cn_b, w_pw, b_pw,
                   gla_g, w_out, final_g):
    row = lambda a: a.reshape(1, -1).astype(F32)
    wlr = jnp.pad(w_in[l, :, OFF_LR:], ((0, 0), (0, LR_PAD - GATE_RANK))).astype(BF16)
    wal = jnp.pad(w_alpha[l], ((0, LR_PAD - GATE_RANK), (0, 0))).astype(BF16)
    cwb = jnp.broadcast_to(conv_w[l][:, None, :], (CONV_W, SUBLANE, D_CONV)).astype(F32)
    return (row(norm_g[l]), w_in, wlr, wal, row(b_alpha[l]), cwb, row(conv_b[l]), row(cn_g[l]),
            row(cn_b[l]), _group_mean_matrix(), w_pw, row(b_pw[l]),
            row(gla_g[l]), w_out, row(final_g))


LAYER_STACKED = (1, 10, 13)


def kernel(x_prompt, x_sample, cache_conv, state_gla, norm_g, w_in, w_alpha, b_alpha, conv_w,
           conv_b, cn_g, cn_b, w_pw, b_pw, gla_g, w_out, final_g):
    depth = w_in.shape[0]
    n_seq, n_dec = x_prompt.shape[0], x_sample.shape[0]
    state_in = state_gla.reshape(depth, n_dec, D_K, HEAD_V)
    hp, hs = x_prompt, x_sample
    prev_p, prev_s = (), ()
    for l in range(depth):
        lw = _layer_weights(l, norm_g, w_in, w_alpha, b_alpha, conv_w, conv_b, cn_g, cn_b,
                            w_pw, b_pw, gla_g, w_out, final_g)
        final = l == depth - 1
        hp, *prev_p = _prompt_layer(l, depth, hp, lw, prev_p, final)
        hs, *prev_s = _sample_layer(l, hs, cache_conv, state_in, lw, prev_s, final)
    conv_p, gla_p = prev_p
    conv_s, gla_s = prev_s
    return (hp, hs, conv_p, gla_p.reshape(depth, n_seq, HEADS, HEAD_K, HEAD_V),
            conv_s, gla_s.reshape(depth, n_dec, HEADS, HEAD_K, HEAD_V))
```

```python
import functools

import jax
import jax.numpy as jnp
import numpy as np
from jax import lax
from jax.experimental import pallas as pl
from jax.experimental.pallas import tpu as pltpu

F32 = jnp.float32
BF16 = jnp.bfloat16

D_MODEL = 1024
D_CONV = 512
CONV_GROUPS = 8
CONV_W = 31
HIST = CONV_W - 1
D_V = 512
HEADS = 4
D_K = 256
HEAD_K = 64
HEAD_V = 128
GATE_RANK = 16
GATE_NORM = 16.0
EPS = 1e-6

LANE = 128
SUBLANE = 8
HIST_PAD = 32
N_LANE_GROUPS = D_CONV // LANE
LR_PAD = LANE

OFF_A, OFF_GL, OFF_ZC = 0, 512, 1024
OFF_Q, OFF_K, OFF_V, OFF_ZG, OFF_LR = 1536, 1792, 2048, 2560, 3072
D_IN = OFF_LR + GATE_RANK
D_IN_PAD = OFF_LR + LR_PAD
PROJ_COLS = OFF_LR - OFF_ZC

PROMPT_TILE = 512
PROMPT_SEG = PROMPT_TILE // SUBLANE
UHALO0 = SUBLANE
UDATA0 = UHALO0 + HIST_PAD
USEG = UDATA0 + PROMPT_SEG
CSEG = PROMPT_SEG + SUBLANE
assert USEG % 32 == 8 and CSEG % 32 == 8
GLA_CHUNK = 128
SAMPLE_SEQS = 16
CAST_ROWS = 64
CONV_UNROLL = 8

VMEM_LIMIT = 56 * 1024 * 1024


def _dot(a, b):
    return jnp.dot(a, b, preferred_element_type=F32)


def _dot_nt(a, b):
    return lax.dot_general(a, b, (((1,), (1,)), ((), ())), preferred_element_type=F32)


def _sigmoid(x):
    return 1.0 / (1.0 + jnp.exp(-x))


def _silu(x):
    return x * _sigmoid(x)


def _log_sigmoid(x):
    return jnp.minimum(x, 0.0) - jnp.log(1.0 + jnp.exp(-jnp.abs(x)))


def _rms_rows(x, g):
    ms = jnp.mean(x * x, axis=-1, keepdims=True)
    return x * lax.rsqrt(ms + EPS) * g


def _split3(g):
    hi = g.astype(BF16)
    r1 = g - hi.astype(F32)
    mid = r1.astype(BF16)
    lo = (r1 - mid.astype(F32)).astype(BF16)
    return hi, mid, lo


def _sum_matmul(m, g):
    hi, mid, lo = _split3(g)
    return _dot(m, hi) + _dot(m, mid) + _dot(m, lo)


def _conv_post(cv, zc, cb_ref, cng_ref, cnb_ref, gm_ref, wpw_ref, bpw_ref):
    cv = cv + cb_ref[...]
    gm = gm_ref[...]
    mu = _dot(cv.astype(BF16), gm)
    d = cv - mu
    var = _dot((d * d).astype(BF16), gm)
    yn = d * lax.rsqrt(var + EPS) * cng_ref[...] + cnb_ref[...]
    pw = _dot(_silu(yn).astype(BF16), wpw_ref[...]) + bpw_ref[...]
    return pw * _silu(zc)


def _gate_log(hb, wlr_ref, wal_ref, bal_ref):
    lr = _dot(hb, wlr_ref[...])
    z = _dot(lr.astype(BF16), wal_ref[...]) + bal_ref[...]
    return _log_sigmoid(z) * (1.0 / GATE_NORM)


def _cast_rows(src_ref, dst_ref):
    k, n = dst_ref.shape

    def body(i, carry):
        rows = pl.ds(pl.multiple_of(i * CAST_ROWS, CAST_ROWS), CAST_ROWS)
        dst_ref[rows, :] = src_ref[rows, 0:n].astype(BF16)
        return carry

    lax.fori_loop(0, k // CAST_ROWS, body, 0)


def _cast_weights(win_ref, wpw_ref, wout_ref, winb, wpwb, woutb):
    _cast_rows(win_ref, winb)
    _cast_rows(wpw_ref, wpwb)
    _cast_rows(wout_ref, woutb)


def _gla_post(o, zg, gg_ref):
    parts = []
    for h in range(HEADS):
        oh = o[:, h * HEAD_V:(h + 1) * HEAD_V]
        ms = jnp.mean(oh * oh, axis=-1, keepdims=True)
        parts.append(oh * lax.rsqrt(ms + EPS) * gg_ref[...])
    return jnp.concatenate(parts, axis=-1) * _silu(zg)


def _finish(x, cat, wout_ref, fg_ref, final):
    y = x + _dot(cat, wout_ref[...])
    if final:
        y = _rms_rows(y, fg_ref[...])
    return y


def _prompt_kernel(x_ref, ng_ref, win_ref, wlr_ref, wal_ref, bal_ref, cwb_ref, cb_ref, cng_ref,
                   cnb_ref, gm_ref, wpw_ref, bpw_ref, gg_ref, wout_ref, fg_ref, ltri_ref, *rest,
                   tt, final):
    (y_ref, buf_ref, sout_ref, ubuf, cvs, proj, os_, cat, sst, winb, wpwb, woutb) = rest[-12:]
    t = pl.program_id(1)

    @pl.when(jnp.logical_and(pl.program_id(0) == 0, t == 0))
    def _():
        _cast_weights(win_ref, wpw_ref, wout_ref, winb, wpwb, woutb)

    seg = tt // SUBLANE

    last = (SUBLANE - 1) * USEG + USEG - HIST_PAD

    @pl.when(t == 0)
    def _():
        ubuf[:, UHALO0:UDATA0, :] = jnp.zeros((N_LANE_GROUPS, HIST_PAD, LANE), F32)
        sst[...] = jnp.zeros_like(sst)

    @pl.when(t > 0)
    def _():
        ubuf[:, UHALO0:UDATA0, :] = ubuf[:, last:last + HIST_PAD, :]

    x = x_ref[0]
    hb = _rms_rows(x, ng_ref[...]).astype(BF16)

    a = _dot(hb, winb[:, OFF_A:OFF_A + D_CONV])
    gl = _dot(hb, winb[:, OFF_GL:OFF_GL + D_CONV])
    u = a * _sigmoid(gl)
    for g in range(N_LANE_GROUPS):
        for j in range(SUBLANE):
            ug = u[:, g * LANE:(g + 1) * LANE]
            ubuf[g, j * USEG + UDATA0:(j + 1) * USEG, :] = ug[j * seg:(j + 1) * seg, :]
            if j > 0:
                ubuf[g, j * USEG + UHALO0:j * USEG + UDATA0, :] = ug[j * seg - HIST_PAD:j * seg, :]
    gk = _gate_log(hb, wlr_ref, wal_ref, bal_ref)

    def project(lo, hi):
        proj[:, lo - OFF_ZC:hi - OFF_ZC] = _dot(hb, winb[:, lo:hi])

    project(OFF_Q, OFF_ZG)

    base = UDATA0 - HIST

    def conv_lane_group(g):
        w = [cwb_ref[k, :, g * LANE:(g + 1) * LANE] for k in range(CONV_W)]
        for i0 in range(0, seg, CONV_UNROLL):
            accs = [None] * CONV_UNROLL
            for m in range(CONV_UNROLL + CONV_W - 1):
                um = ubuf[g, pl.ds(i0 + base + m, SUBLANE, stride=USEG), :]
                for ii in range(CONV_UNROLL):
                    k = m - ii
                    if 0 <= k < CONV_W:
                        accs[ii] = um * w[k] if accs[ii] is None else accs[ii] + um * w[k]
            for ii in range(CONV_UNROLL):
                cvs[g, pl.ds(i0 + ii, SUBLANE, stride=CSEG), :] = accs[ii]

    c = GLA_CHUNK
    half = c // 2
    row = lax.broadcasted_iota(jnp.int32, (c, c), 0)
    col = lax.broadcasted_iota(jnp.int32, (c, c), 1)
    causal = col <= row
    head_lo = col < HEAD_K
    ltri = ltri_ref[...]

    def attention_block(i, s_old):
        rows = slice(i * c, (i + 1) * c)
        gcum = _sum_matmul(ltri, gk[rows, :])
        q = proj[rows, OFF_Q - OFF_ZC:OFF_K - OFF_ZC] * (HEAD_K ** -0.5)
        k = proj[rows, OFF_K - OFF_ZC:OFF_V - OFF_ZC]
        v = proj[rows, OFF_V - OFF_ZC:OFF_ZG - OFF_ZC].astype(BF16)
        mid = gcum[half - 1:half, :]
        xq = q * jnp.exp(gcum - mid)
        yk = (k * jnp.exp(mid - gcum)).astype(BF16)
        qg = q * jnp.exp(gcum)
        gt = gcum.T
        glast = gt[:, c - 1:c]
        kdt = (k.T * jnp.exp(glast - gt)).astype(BF16)
        dcol = jnp.exp(glast)
        s_bf = s_old.astype(BF16)
        outs = []
        s_new = []
        for h in range(HEADS):
            j, hi = divmod(h, 2)
            pair = slice(j * LANE, (j + 1) * LANE)
            hm = head_lo if hi == 0 else jnp.logical_not(head_lo)
            xm = jnp.where(hm, xq[:, pair], 0.0).astype(BF16)
            qm = jnp.where(hm, qg[:, pair], 0.0).astype(BF16)
            att = jnp.where(causal, _dot_nt(xm, yk[:, pair]), 0.0).astype(BF16)
            vh = v[:, h * HEAD_V:(h + 1) * HEAD_V]
            outs.append(_dot(att, vh) + _dot(qm, s_bf[pair, :]))
            hr = slice(h * HEAD_K, (h + 1) * HEAD_K)
            s_new.append(dcol[hr, :] * s_old[hr, :] + _dot(kdt[hr, :], vh))
        os_[rows, :] = jnp.concatenate(outs, axis=-1)
        return jnp.concatenate(s_new, axis=0)

    fill_cols = D_CONV // 2
    fillers = [(lo, lo + fill_cols) for lo in (OFF_ZC, OFF_ZC + fill_cols, OFF_ZG, OFF_ZG + fill_cols)]
    s_old = sst[...]
    for n in range(max(N_LANE_GROUPS, tt // c, len(fillers))):
        if n < N_LANE_GROUPS:
            conv_lane_group(n)
        if n < tt // c:
            s_old = attention_block(n, s_old)
        if n < len(fillers):
            project(*fillers[n])
    sst[...] = s_old

    cv = jnp.concatenate(
        [jnp.concatenate([cvs[g, j * CSEG:j * CSEG + seg, :] for j in range(SUBLANE)], axis=0)
         for g in range(N_LANE_GROUPS)], axis=-1)
    zc = proj[:, OFF_ZC - OFF_ZC:OFF_Q - OFF_ZC]
    cvo = _conv_post(cv, zc, cb_ref, cng_ref, cnb_ref, gm_ref, wpwb, bpw_ref)
    cat[:, 0:D_CONV] = cvo.astype(BF16)

    zg = proj[:, OFF_ZG - OFF_ZC:OFF_LR - OFF_ZC]
    cat[:, D_CONV:] = _gla_post(os_[...], zg, gg_ref).astype(BF16)
    y_ref[0] = _finish(x, cat[...], woutb, fg_ref, final)

    @pl.when(t == pl.num_programs(1) - 1)
    def _():
        buf_ref[0] = jnp.concatenate(
            [ubuf[g, last + HIST_PAD - HIST:last + HIST_PAD, :] for g in range(N_LANE_GROUPS)],
            axis=-1)
        sout_ref[0] = sst[...]


def _const_spec(a):
    return pl.BlockSpec(a.shape, lambda *_: (0,) * a.ndim, pipeline_mode=pl.Buffered(1))


def _layer_spec(a, l):
    return pl.BlockSpec((None,) + a.shape[1:], lambda *_: (l,) + (0,) * (a.ndim - 1),
                        pipeline_mode=pl.Buffered(1))


def _operands(l, lw, masks, prev):
    ops = list(lw) + list(masks) + list(prev)
    specs = [_layer_spec(a, l) if i in LAYER_STACKED else _const_spec(a) for i, a in enumerate(lw)]
    specs += [_const_spec(m) for m in masks]
    specs += [pl.BlockSpec(memory_space=pl.ANY) for _ in prev]
    return ops, specs


def _weight_scratch():
    return [pltpu.VMEM((D_MODEL, OFF_LR), BF16),
            pltpu.VMEM((D_CONV, D_CONV), BF16),
            pltpu.VMEM((D_MODEL, D_MODEL), BF16)]


def _prompt_layer(l, depth, x, lw, prev, final):
    b, t, _ = x.shape
    tt = PROMPT_TILE
    assert t % tt == 0 and tt % GLA_CHUNK == 0 and tt >= HIST_PAD
    assert tt == PROMPT_TILE and tt % (SUBLANE * CONV_UNROLL) == 0
    ops, specs = _operands(l, lw, (_lower_tri(GLA_CHUNK),), prev)
    in_specs = [pl.BlockSpec((1, tt, D_MODEL), lambda i, j: (i, j, 0))] + specs
    out_shape = (jax.ShapeDtypeStruct((b, t, D_MODEL), F32),
                 jax.ShapeDtypeStruct((depth, b, HIST, D_CONV), F32),
                 jax.ShapeDtypeStruct((depth, b, D_K, HEAD_V), F32))
    out_specs = (pl.BlockSpec((1, tt, D_MODEL), lambda i, j: (i, j, 0)),
                 pl.BlockSpec((None, 1, HIST, D_CONV), lambda i, j: (l, i, 0, 0)),
                 pl.BlockSpec((None, 1, D_K, HEAD_V), lambda i, j: (l, i, 0, 0)))
    n_in = 1 + len(ops)
    aliases = {n_in - len(prev) + k: 1 + k for k in range(len(prev))}
    scratch = [pltpu.VMEM((N_LANE_GROUPS, SUBLANE * USEG, LANE), F32),
               pltpu.VMEM((N_LANE_GROUPS, SUBLANE * CSEG, LANE), F32),
               pltpu.VMEM((tt, PROJ_COLS), F32),
               pltpu.VMEM((tt, D_V), F32),
               pltpu.VMEM((tt, D_MODEL), BF16),
               pltpu.VMEM((D_K, HEAD_V), F32)]
    return pl.pallas_call(
        functools.partial(_prompt_kernel, tt=tt, final=final),
        grid=(b, t // tt),
        in_specs=in_specs, out_specs=out_specs, out_shape=out_shape,
        scratch_shapes=scratch + _weight_scratch(),
        input_output_aliases=aliases,
        compiler_params=pltpu.CompilerParams(
            dimension_semantics=("arbitrary", "arbitrary"), vmem_limit_bytes=VMEM_LIMIT),
        name="prompt_layer",
    )(x, *ops)


def _sample_kernel(x_ref, cache_ref, s0_ref, ng_ref, win_ref, wlr_ref, wal_ref, bal_ref, cwb_ref,
                   cb_ref, cng_ref, cnb_ref, gm_ref, wpw_ref, bpw_ref, gg_ref, wout_ref, fg_ref,
                   ltri_ref, ball_ref, *rest, nb, ts, final):
    (y_ref, buf_ref, sout_ref, full, winb, wpwb, woutb) = rest[-7:]

    @pl.when(pl.program_id(0) == 0)
    def _():
        _cast_weights(win_ref, wpw_ref, wout_ref, winb, wpwb, woutb)

    r = nb * ts
    x = x_ref[...].reshape(r, D_MODEL)
    hb = _rms_rows(x, ng_ref[...]).astype(BF16)

    a = _dot(hb, winb[:, OFF_A:OFF_A + D_CONV])
    gl = _dot(hb, winb[:, OFF_GL:OFF_GL + D_CONV])
    full[:, 0:HIST, :] = cache_ref[...]
    full[:, HIST:HIST + ts, :] = (a * _sigmoid(gl)).reshape(nb, ts, D_CONV)
    acc = full[:, 0:ts, :] * cwb_ref[0]
    for k in range(1, CONV_W):
        acc = acc + full[:, k:k + ts, :] * cwb_ref[k]
    buf_ref[...] = full[:, ts:ts + HIST, :]
    zc = _dot(hb, winb[:, OFF_ZC:OFF_ZC + D_CONV])
    cvo = _conv_post(acc.reshape(r, D_CONV), zc, cb_ref, cng_ref, cnb_ref, gm_ref,
                     wpwb, bpw_ref)

    q = _dot(hb, winb[:, OFF_Q:OFF_Q + D_K]) * (HEAD_K ** -0.5)
    k = _dot(hb, winb[:, OFF_K:OFF_K + D_K])
    v = _dot(hb, winb[:, OFF_V:OFF_V + D_V]).astype(BF16)
    g = _gate_log(hb, wlr_ref, wal_ref, bal_ref)
    gcum = _sum_matmul(ltri_ref[...], g)
    gtot = _sum_matmul(ball_ref[...], g)
    xq = q * jnp.exp(gcum)
    yk = (k * jnp.exp(-gcum)).astype(BF16)
    kdt = (k * jnp.exp(gtot - gcum)).T.astype(BF16)
    gt = g.T

    row = lax.broadcasted_iota(jnp.int32, (r, r), 0)
    col = lax.broadcasted_iota(jnp.int32, (r, r), 1)
    causal = jnp.logical_and(col <= row, (col // ts) == (row // ts))
    head_lo = lax.broadcasted_iota(jnp.int32, (r, LANE), 1) < HEAD_K
    seq_sel = (lax.broadcasted_iota(jnp.int32, (nb, 1, r), 2) // ts
               == lax.broadcasted_iota(jnp.int32, (nb, 1, r), 0))
    blk = (lax.broadcasted_iota(jnp.int32, (r, nb * LANE), 1) // LANE
           == lax.broadcasted_iota(jnp.int32, (r, nb * LANE), 0) // ts)

    s0 = s0_ref[...]
    s0_bf = s0.astype(BF16)
    gsum = jnp.sum(jnp.where(seq_sel, gt[None, :, :], 0.0), axis=-1, keepdims=True)
    s_new = jnp.exp(gsum) * s0

    outs = []
    kvs = []
    for h in range(HEADS):
        j, hi = divmod(h, 2)
        pair = slice(j * LANE, (j + 1) * LANE)
        hm = head_lo if hi == 0 else jnp.logical_not(head_lo)
        xm = jnp.where(hm, xq[:, pair], 0.0).astype(BF16)
        att = jnp.where(causal, _dot_nt(xm, yk[:, pair]), 0.0).astype(BF16)
        vh = v[:, h * HEAD_V:(h + 1) * HEAD_V]
        xblk = jnp.where(blk, jnp.concatenate([xm] * nb, axis=-1), jnp.zeros((), BF16))
        s_pair = s0_bf[:, pair, :].reshape(nb * LANE, HEAD_V)
        outs.append(_dot(att, vh) + _dot(xblk, s_pair))
        kh = kdt[h * HEAD_K:(h + 1) * HEAD_K, :]
        kblk = jnp.where(seq_sel, kh[None, :, :], jnp.zeros((), BF16)).reshape(nb * HEAD_K, r)
        kvs.append(_dot(kblk, vh).reshape(nb, HEAD_K, HEAD_V))
    sout_ref[...] = s_new + jnp.concatenate(kvs, axis=1)

    o = jnp.concatenate(outs, axis=-1)
    zg = _dot(hb, winb[:, OFF_ZG:OFF_ZG + D_V])
    cat = jnp.concatenate([cvo.astype(BF16), _gla_post(o, zg, gg_ref).astype(BF16)], axis=-1)
    y_ref[...] = _finish(x, cat, woutb, fg_ref, final).reshape(nb, ts, D_MODEL)


def _sample_layer(l, x, cache, s0, lw, prev, final):
    b, ts, _ = x.shape
    depth = cache.shape[0]
    nb = SAMPLE_SEQS
    assert b % nb == 0 and ts == SUBLANE and nb * ts == LANE
    r = nb * ts
    ops, specs = _operands(l, lw, (_block_tri(r, ts), _block_ones(r, ts)), prev)
    state_specs = [pl.BlockSpec((None, nb, HIST, D_CONV), lambda i: (l, i, 0, 0)),
                   pl.BlockSpec((None, nb, D_K, HEAD_V), lambda i: (l, i, 0, 0))]
    in_specs = [pl.BlockSpec((nb, ts, D_MODEL), lambda i: (i, 0, 0))] + state_specs + specs
    out_shape = (jax.ShapeDtypeStruct((b, ts, D_MODEL), F32),
                 jax.ShapeDtypeStruct(cache.shape, F32),
                 jax.ShapeDtypeStruct(s0.shape, F32))
    out_specs = [pl.BlockSpec((nb, ts, D_MODEL), lambda i: (i, 0, 0))] + state_specs
    n_in = 3 + len(ops)
    aliases = {n_in - len(prev) + k: 1 + k for k in range(len(prev))}
    return pl.pallas_call(
        functools.partial(_sample_kernel, nb=nb, ts=ts, final=final),
        grid=(b // nb,),
        in_specs=in_specs, out_specs=out_specs, out_shape=out_shape,
        scratch_shapes=[pltpu.VMEM((nb, HIST + ts + 2, D_CONV), F32)] + _weight_scratch(),
        input_output_aliases=aliases,
        compiler_params=pltpu.CompilerParams(
            dimension_semantics=("arbitrary",), vmem_limit_bytes=VMEM_LIMIT),
        name="sample_layer",
    )(x, cache, s0, *ops)


def _lower_tri(n):
    return jnp.asarray(np.tril(np.ones((n, n), np.float32)), BF16)


def _block_tri(n, blk):
    i = np.arange(n)
    m = (i[None, :] <= i[:, None]) & (i[None, :] // blk == i[:, None] // blk)
    return jnp.asarray(m.astype(np.float32), BF16)


def _block_ones(n, blk):
    i = np.arange(n)
    return jnp.asarray((i[None, :] // blk == i[:, None] // blk).astype(np.float32), BF16)


def _group_mean_matrix():
    gsz = D_CONV // CONV_GROUPS
    i = np.arange(D_CONV)
    m = (i[None, :] // gsz == i[:, None] // gsz).astype(np.float32) / gsz
    return jnp.asarray(m, BF16)


def _layer_weights(l, norm_g, w_in, w_alpha, b_alpha, conv_w, conv_b, cn_g, cn_b, w_pw, b_pw,
                   gla_g, w_out, final_g):
    row = lambda a: a.reshape(1, -1).astype(F32)
    wlr = jnp.pad(w_in[l, :, OFF_LR:], ((0, 0), (0, LR_PAD - GATE_RANK))).astype(BF16)
    wal = jnp.pad(w_alpha[l], ((0, LR_PAD - GATE_RANK), (0, 0))).astype(BF16)
    cwb = jnp.broadcast_to(conv_w[l][:, None, :], (CONV_W, SUBLANE, D_CONV)).astype(F32)
    return (row(norm_g[l]), w_in, wlr, wal, row(b_alpha[l]), cwb, row(conv_b[l]), row(cn_g[l]),
            row(cn_b[l]), _group_mean_matrix(), w_pw, row(b_pw[l]),
            row(gla_g[l]), w_out, row(final_g))


LAYER_STACKED = (1, 10, 13)


def kernel(x_prompt, x_sample, cache_conv, state_gla, norm_g, w_in, w_alpha, b_alpha, conv_w,
           conv_b, cn_g, cn_b, w_pw, b_pw, gla_g, w_out, final_g):
    depth = w_in.shape[0]
    n_seq, n_dec = x_prompt.shape[0], x_sample.shape[0]
    state_in = state_gla.reshape(depth, n_dec, D_K, HEAD_V)
    hp, hs = x_prompt, x_sample
    prev_p, prev_s = (), ()
    for l in range(depth):
        lw = _layer_weights(l, norm_g, w_in, w_alpha, b_alpha, conv_w, conv_b, cn_g, cn_b,
                            w_pw, b_pw, gla_g, w_out, final_g)
        final = l == depth - 1
        hp, *prev_p = _prompt_layer(l, depth, hp, lw, prev_p, final)
        hs, *prev_s = _sample_layer(l, hs, cache_conv, state_in, lw, prev_s, final)
    conv_p, gla_p = prev_p
    conv_s, gla_s = prev_s
    return (hp, hs, conv_p, gla_p.reshape(depth, n_seq, HEADS, HEAD_K, HEAD_V),
            conv_s, gla_s.reshape(depth, n_dec, HEADS, HEAD_K, HEAD_V))
```

```python
import functools

import jax
import jax.numpy as jnp
import numpy as np
from jax import lax
from jax.experimental import pallas as pl
from jax.experimental.pallas import tpu as pltpu

F32 = jnp.float32
BF16 = jnp.bfloat16

D_MODEL = 1024
D_CONV = 512
CONV_GROUPS = 8
CONV_W = 31
HIST = CONV_W - 1
D_V = 512
HEADS = 4
D_K = 256
HEAD_K = 64
HEAD_V = 128
GATE_RANK = 16
GATE_NORM = 16.0
EPS = 1e-6

LANE = 128
SUBLANE = 8
HIST_PAD = 32
N_LANE_GROUPS = D_CONV // LANE
LR_PAD = LANE

OFF_A, OFF_GL, OFF_ZC = 0, 512, 1024
OFF_Q, OFF_K, OFF_V, OFF_ZG, OFF_LR = 1536, 1792, 2048, 2560, 3072
PROJ_COLS = OFF_LR - OFF_ZC

GLA_CHUNK = 128
PROMPT_TILE = 512
PROMPT_SEQS = 1
PROMPT_SEG = PROMPT_TILE // SUBLANE
UHALO0 = SUBLANE
UDATA0 = UHALO0 + HIST_PAD
USEG = UDATA0 + PROMPT_SEG
CSEG = PROMPT_SEG + SUBLANE
assert (USEG // SUBLANE) % 2 == 1 and (CSEG // SUBLANE) % 2 == 1
UTAIL = SUBLANE * USEG - HIST_PAD
SAMPLE_SEQS = 16
CONV_UNROLL = 8
WPREP_ROWS = 512

VMEM_LIMIT = 56 * 1024 * 1024


def _dot(a, b):
    return jnp.dot(a, b, preferred_element_type=F32)


def _dot_nt(a, b):
    return lax.dot_general(a, b, (((1,), (1,)), ((), ())), preferred_element_type=F32)


def _sigmoid(x):
    return 1.0 / (1.0 + jnp.exp(-x))


def _silu(x):
    return x * _sigmoid(x)


def _log_sigmoid(x):
    return jnp.minimum(x, 0.0) - jnp.log(1.0 + jnp.exp(-jnp.abs(x)))


def _rms_rows(x, g):
    ms = jnp.mean(x * x, axis=-1, keepdims=True)
    return x * lax.rsqrt(ms + EPS) * g


def _split3(g):
    hi = g.astype(BF16)
    r1 = g - hi.astype(F32)
    mid = r1.astype(BF16)
    lo = (r1 - mid.astype(F32)).astype(BF16)
    return hi, mid, lo


def _sum_matmul(m, g):
    hi, mid, lo = _split3(g)
    return _dot(m, hi) + _dot(m, mid) + _dot(m, lo)


def _conv_post(cv, zc, cb_ref, cng_ref, cnb_ref, gm_ref, wpw_ref, bpw_ref):
    cv = cv + cb_ref[...]
    gm = gm_ref[...]
    mu = _dot(cv.astype(BF16), gm)
    d = cv - mu
    var = _dot((d * d).astype(BF16), gm)
    yn = d * lax.rsqrt(var + EPS) * cng_ref[...] + cnb_ref[...]
    pw = _dot(_silu(yn).astype(BF16), wpw_ref[...]) + bpw_ref[...]
    return pw * _silu(zc)


def _gate_log(hb, wlrt_ref, wal_ref, bal_ref):
    lr = _dot_nt(hb, wlrt_ref[...])
    z = _dot(lr.astype(BF16), wal_ref[...]) + bal_ref[...]
    return _log_sigmoid(z) * (1.0 / GATE_NORM)


def _gla_post(o, zg, gg_ref):
    parts = []
    for h in range(HEADS):
        oh = o[:, h * HEAD_V:(h + 1) * HEAD_V]
        ms = jnp.mean(oh * oh, axis=-1, keepdims=True)
        parts.append(oh * lax.rsqrt(ms + EPS) * gg_ref[...])
    return jnp.concatenate(parts, axis=-1) * _silu(zg)


def _finish(x, cat, wout_ref, fg_ref, final):
    y = x + _dot(cat, wout_ref[...])
    if final:
        y = _rms_rows(y, fg_ref[...])
    return y


def _wprep_kernel(wt_ref, lrt_ref, o_ref, olr_ref):
    o_ref[...] = wt_ref[...].T.astype(BF16)

    @pl.when(pl.program_id(1) == 0)
    def _():
        olr_ref[...] = jnp.zeros_like(olr_ref)
        olr_ref[0:GATE_RANK, :] = lrt_ref[...].astype(BF16)


def _input_projection_bf16(w_in):
    depth = w_in.shape[0]
    wt = jnp.swapaxes(w_in, 1, 2)
    return pl.pallas_call(
        _wprep_kernel,
        grid=(depth, OFF_LR // WPREP_ROWS),
        in_specs=[pl.BlockSpec((None, WPREP_ROWS, D_MODEL), lambda l, c: (l, c, 0)),
                  pl.BlockSpec((None, GATE_RANK, D_MODEL), lambda l, c: (l, OFF_LR // GATE_RANK, 0))],
        out_specs=[pl.BlockSpec((None, D_MODEL, WPREP_ROWS), lambda l, c: (l, 0, c)),
                   pl.BlockSpec((None, LR_PAD, D_MODEL), lambda l, c: (l, 0, 0))],
        out_shape=[jax.ShapeDtypeStruct((depth, D_MODEL, OFF_LR), BF16),
                   jax.ShapeDtypeStruct((depth, LR_PAD, D_MODEL), BF16)],
        compiler_params=pltpu.CompilerParams(dimension_semantics=("arbitrary", "arbitrary")),
        name="weight_layout",
    )(wt, wt)


def _prompt_kernel(x_ref, ng_ref, winb, wlrt_ref, wal_ref, bal_ref, cwb_ref, cb_ref, cng_ref,
                   cnb_ref, gm_ref, wpwb, bpw_ref, gg_ref, woutb, fg_ref, ltri_ref, *rest,
                   ns, tt, final):
    (y_ref, buf_ref, sout_ref, ubuf, cvs, proj, os_, cat, sst) = rest[-9:]
    t = pl.program_id(1)
    seg = tt // SUBLANE
    c = GLA_CHUNK
    half = c // 2

    @pl.when(t == 0)
    def _():
        ubuf[:, :, UTAIL:UTAIL + HIST_PAD, :] = jnp.zeros((ns, N_LANE_GROUPS, HIST_PAD, LANE), F32)
        sst[...] = jnp.zeros_like(sst)

    x = x_ref[...].reshape(ns * tt, D_MODEL)
    hb = _rms_rows(x, ng_ref[...]).astype(BF16)

    a = _dot(hb, winb[:, OFF_A:OFF_A + D_CONV])
    gl = _dot(hb, winb[:, OFF_GL:OFF_GL + D_CONV])
    u = a * _sigmoid(gl)
    for s in range(ns):
        for g in range(N_LANE_GROUPS):
            prev_tail = ubuf[s, g, UTAIL:UTAIL + HIST_PAD, :]
            ext = jnp.concatenate([prev_tail, u[s * tt:(s + 1) * tt, g * LANE:(g + 1) * LANE]], axis=0)
            for j in range(SUBLANE):
                ubuf[s, g, j * USEG + UHALO0:(j + 1) * USEG, :] = ext[j * seg:(j + 1) * seg + HIST_PAD, :]
    gk = _gate_log(hb, wlrt_ref, wal_ref, bal_ref)

    def project(lo, hi):
        proj[:, lo - OFF_ZC:hi - OFF_ZC] = _dot(hb, winb[:, lo:hi])

    project(OFF_Q, OFF_ZG)

    base = UDATA0 - HIST

    def conv_lane_group(g, s):
        w = [cwb_ref[k, :, g * LANE:(g + 1) * LANE] for k in range(CONV_W)]
        for i0 in range(0, seg, CONV_UNROLL):
            accs = [None] * CONV_UNROLL
            for m in range(CONV_UNROLL + CONV_W - 1):
                um = ubuf[s, g, pl.ds(i0 + base + m, SUBLANE, stride=USEG), :]
                for ii in range(CONV_UNROLL):
                    k = m - ii
                    if 0 <= k < CONV_W:
                        accs[ii] = um * w[k] if accs[ii] is None else accs[ii] + um * w[k]
            for ii in range(CONV_UNROLL):
                cvs[s, g, pl.ds(i0 + ii, SUBLANE, stride=CSEG), :] = accs[ii]

    row = lax.broadcasted_iota(jnp.int32, (c, c), 0)
    col = lax.broadcasted_iota(jnp.int32, (c, c), 1)
    causal = col <= row
    head_lo = col < HEAD_K
    ltri = ltri_ref[...]

    def attention_block(r0, s_old):
        rows = slice(r0, r0 + c)
        gcum = _sum_matmul(ltri, gk[rows, :])
        q = proj[rows, OFF_Q - OFF_ZC:OFF_K - OFF_ZC] * (HEAD_K ** -0.5)
        k = proj[rows, OFF_K - OFF_ZC:OFF_V - OFF_ZC]
        v = proj[rows, OFF_V - OFF_ZC:OFF_ZG - OFF_ZC].astype(BF16)
        mid = gcum[half - 1:half, :]
        xq = q * jnp.exp(gcum - mid)
        yk = (k * jnp.exp(mid - gcum)).astype(BF16)
        qg = q * jnp.exp(gcum)
        gt = gcum.T
        glast = gt[:, c - 1:c]
        kdt = (k.T * jnp.exp(glast - gt)).astype(BF16)
        dcol = jnp.exp(glast)
        s_bf = s_old.astype(BF16)
        outs = []
        s_new = []
        for h in range(HEADS):
            j, hi = divmod(h, 2)
            pair = slice(j * LANE, (j + 1) * LANE)
            hm = head_lo if hi == 0 else jnp.logical_not(head_lo)
            xm = jnp.where(hm, xq[:, pair], 0.0).astype(BF16)
            qm = jnp.where(hm, qg[:, pair], 0.0).astype(BF16)
            att = jnp.where(causal, _dot_nt(xm, yk[:, pair]), 0.0).astype(BF16)
            vh = v[:, h * HEAD_V:(h + 1) * HEAD_V]
            outs.append(_dot(att, vh) + _dot(qm, s_bf[pair, :]))
            hr = slice(h * HEAD_K, (h + 1) * HEAD_K)
            s_new.append(dcol[hr, :] * s_old[hr, :] + _dot(kdt[hr, :], vh))
        os_[rows, :] = jnp.concatenate(outs, axis=-1)
        return jnp.concatenate(s_new, axis=0)

    fill_cols = D_CONV // 2
    fillers = [(lo, lo + fill_cols) for lo in (OFF_ZC, OFF_ZC + fill_cols, OFF_ZG, OFF_ZG + fill_cols)]
    conv_items = [(g, s) for s in range(ns) for g in range(N_LANE_GROUPS)]
    blocks = [(s, i) for s in range(ns) for i in range(tt // c)]
    states = [sst[s] for s in range(ns)]
    for n in range(max(len(conv_items), len(blocks), len(fillers))):
        if n < len(conv_items):
            conv_lane_group(*conv_items[n])
        if n < len(blocks):
            s, i = blocks[n]
            states[s] = attention_block(s * tt + i * c, states[s])
        if n < len(fillers):
            project(*fillers[n])
    for s in range(ns):
        sst[s] = states[s]

    cv = jnp.concatenate(
        [jnp.concatenate(
            [jnp.concatenate([cvs[s, g, j * CSEG:j * CSEG + seg, :] for j in range(SUBLANE)], axis=0)
             for g in range(N_LANE_GROUPS)], axis=-1) for s in range(ns)], axis=0)
    zc = proj[:, 0:OFF_Q - OFF_ZC]
    cvo = _conv_post(cv, zc, cb_ref, cng_ref, cnb_ref, gm_ref, wpwb, bpw_ref)
    cat[:, 0:D_CONV] = cvo.astype(BF16)

    zg = proj[:, OFF_ZG - OFF_ZC:OFF_LR - OFF_ZC]
    cat[:, D_CONV:] = _gla_post(os_[...], zg, gg_ref).astype(BF16)
    y_ref[...] = _finish(x, cat[...], woutb, fg_ref, final).reshape(ns, tt, D_MODEL)

    @pl.when(t == pl.num_programs(1) - 1)
    def _():
        for s in range(ns):
            buf_ref[s] = jnp.concatenate(
                [ubuf[s, g, UTAIL + HIST_PAD - HIST:UTAIL + HIST_PAD, :]
                 for g in range(N_LANE_GROUPS)], axis=-1)
        sout_ref[...] = sst[...]


def _const_spec(a):
    return pl.BlockSpec(a.shape, lambda *_: (0,) * a.ndim, pipeline_mode=pl.Buffered(1))


def _layer_spec(a, l):
    return pl.BlockSpec((None,) + a.shape[1:], lambda *_: (l,) + (0,) * (a.ndim - 1),
                        pipeline_mode=pl.Buffered(1))


def _operands(l, lw, masks, prev):
    ops = list(lw) + list(masks) + list(prev)
    specs = [_layer_spec(a, l) if i in LAYER_STACKED else _const_spec(a) for i, a in enumerate(lw)]
    specs += [_const_spec(m) for m in masks]
    specs += [pl.BlockSpec(memory_space=pl.ANY) for _ in prev]
    return ops, specs


def _prompt_layer(l, depth, x, lw, prev, final):
    b, t, _ = x.shape
    ns, tt = PROMPT_SEQS, PROMPT_TILE
    assert b % ns == 0 and t % tt == 0 and tt % GLA_CHUNK == 0
    assert tt % (SUBLANE * CONV_UNROLL) == 0 and t >= HIST_PAD
    ops, specs = _operands(l, lw, (_lower_tri(GLA_CHUNK),), prev)
    in_specs = [pl.BlockSpec((ns, tt, D_MODEL), lambda i, j: (i, j, 0))] + specs
    out_shape = (jax.ShapeDtypeStruct((b, t, D_MODEL), F32),
                 jax.ShapeDtypeStruct((depth, b, HIST, D_CONV), F32),
                 jax.ShapeDtypeStruct((depth, b, D_K, HEAD_V), F32))
    out_specs = (pl.BlockSpec((ns, tt, D_MODEL), lambda i, j: (i, j, 0)),
                 pl.BlockSpec((None, ns, HIST, D_CONV), lambda i, j: (l, i, 0, 0)),
                 pl.BlockSpec((None, ns, D_K, HEAD_V), lambda i, j: (l, i, 0, 0)))
    n_in = 1 + len(ops)
    aliases = {n_in - len(prev) + k: 1 + k for k in range(len(prev))}
    scratch = [pltpu.VMEM((ns, N_LANE_GROUPS, SUBLANE * USEG, LANE), F32),
               pltpu.VMEM((ns, N_LANE_GROUPS, SUBLANE * CSEG, LANE), F32),
               pltpu.VMEM((ns * tt, PROJ_COLS), F32),
               pltpu.VMEM((ns * tt, D_V), F32),
               pltpu.VMEM((ns * tt, D_MODEL), BF16),
               pltpu.VMEM((ns, D_K, HEAD_V), F32)]
    return pl.pallas_call(
        functools.partial(_prompt_kernel, ns=ns, tt=tt, final=final),
        grid=(b // ns, t // tt),
        in_specs=in_specs, out_specs=out_specs, out_shape=out_shape,
        scratch_shapes=scratch,
        input_output_aliases=aliases,
        compiler_params=pltpu.CompilerParams(
            dimension_semantics=("arbitrary", "arbitrary"), vmem_limit_bytes=VMEM_LIMIT),
        name="prompt_layer",
    )(x, *ops)


def _sample_kernel(x_ref, cache_ref, s0_ref, ng_ref, winb, wlrt_ref, wal_ref, bal_ref, cwb_ref,
                   cb_ref, cng_ref, cnb_ref, gm_ref, wpwb, bpw_ref, gg_ref, woutb, fg_ref,
                   ltri_ref, ball_ref, *rest, nb, ts, final):
    (y_ref, buf_ref, sout_ref, full) = rest[-4:]
    r = nb * ts
    x = x_ref[...].reshape(r, D_MODEL)
    hb = _rms_rows(x, ng_ref[...]).astype(BF16)

    a = _dot(hb, winb[:, OFF_A:OFF_A + D_CONV])
    gl = _dot(hb, winb[:, OFF_GL:OFF_GL + D_CONV])
    full[:, 0:HIST, :] = cache_ref[...]
    full[:, HIST:HIST + ts, :] = (a * _sigmoid(gl)).reshape(nb, ts, D_CONV)
    acc = full[:, 0:ts, :] * cwb_ref[0]
    for k in range(1, CONV_W):
        acc = acc + full[:, k:k + ts, :] * cwb_ref[k]
    buf_ref[...] = full[:, ts:ts + HIST, :]
    zc = _dot(hb, winb[:, OFF_ZC:OFF_ZC + D_CONV])
    cvo = _conv_post(acc.reshape(r, D_CONV), zc, cb_ref, cng_ref, cnb_ref, gm_ref,
                     wpwb, bpw_ref)

    q = _dot(hb, winb[:, OFF_Q:OFF_Q + D_K]) * (HEAD_K ** -0.5)
    k = _dot(hb, winb[:, OFF_K:OFF_K + D_K])
    v = _dot(hb, winb[:, OFF_V:OFF_V + D_V]).astype(BF16)
    g = _gate_log(hb, wlrt_ref, wal_ref, bal_ref)
    gcum = _sum_matmul(ltri_ref[...], g)
    gtot = _sum_matmul(ball_ref[...], g)
    xq = q * jnp.exp(gcum)
    yk = (k * jnp.exp(-gcum)).astype(BF16)
    kdt = (k * jnp.exp(gtot - gcum)).T.astype(BF16)
    gt = g.T

    row = lax.broadcasted_iota(jnp.int32, (r, r), 0)
    col = lax.broadcasted_iota(jnp.int32, (r, r), 1)
    causal = jnp.logical_and(col <= row, (col // ts) == (row // ts))
    head_lo = lax.broadcasted_iota(jnp.int32, (r, LANE), 1) < HEAD_K
    seq_sel = (lax.broadcasted_iota(jnp.int32, (nb, 1, r), 2) // ts
               == lax.broadcasted_iota(jnp.int32, (nb, 1, r), 0))
    blk = (lax.broadcasted_iota(jnp.int32, (r, nb * LANE), 1) // LANE
           == lax.broadcasted_iota(jnp.int32, (r, nb * LANE), 0) // ts)

    s0 = s0_ref[...]
    s0_bf = s0.astype(BF16)
    gsum = jnp.sum(jnp.where(seq_sel, gt[None, :, :], 0.0), axis=-1, keepdims=True)
    s_new = jnp.exp(gsum) * s0

    outs = []
    kvs = []
    for h in range(HEADS):
        j, hi = divmod(h, 2)
        pair = slice(j * LANE, (j + 1) * LANE)
        hm = head_lo if hi == 0 else jnp.logical_not(head_lo)
        xm = jnp.where(hm, xq[:, pair], 0.0).astype(BF16)
        att = jnp.where(causal, _dot_nt(xm, yk[:, pair]), 0.0).astype(BF16)
        vh = v[:, h * HEAD_V:(h + 1) * HEAD_V]
        xblk = jnp.where(blk, jnp.concatenate([xm] * nb, axis=-1), jnp.zeros((), BF16))
        s_pair = s0_bf[:, pair, :].reshape(nb * LANE, HEAD_V)
        outs.append(_dot(att, vh) + _dot(xblk, s_pair))
        kh = kdt[h * HEAD_K:(h + 1) * HEAD_K, :]
        kblk = jnp.where(seq_sel, kh[None, :, :], jnp.zeros((), BF16)).reshape(nb * HEAD_K, r)
        kvs.append(_dot(kblk, vh).reshape(nb, HEAD_K, HEAD_V))
    sout_ref[...] = s_new + jnp.concatenate(kvs, axis=1)

    o = jnp.concatenate(outs, axis=-1)
    zg = _dot(hb, winb[:, OFF_ZG:OFF_ZG + D_V])
    cat = jnp.concatenate([cvo.astype(BF16), _gla_post(o, zg, gg_ref).astype(BF16)], axis=-1)
    y_ref[...] = _finish(x, cat, woutb, fg_ref, final).reshape(nb, ts, D_MODEL)


def _sample_layer(l, x, cache, s0, lw, prev, final):
    b, ts, _ = x.shape
    nb = SAMPLE_SEQS
    assert b % nb == 0 and ts == SUBLANE and nb * ts == LANE
    r = nb * ts
    ops, specs = _operands(l, lw, (_block_tri(r, ts), _block_ones(r, ts)), prev)
    state_specs = [pl.BlockSpec((None, nb, HIST, D_CONV), lambda i: (l, i, 0, 0)),
                   pl.BlockSpec((None, nb, D_K, HEAD_V), lambda i: (l, i, 0, 0))]
    in_specs = [pl.BlockSpec((nb, ts, D_MODEL), lambda i: (i, 0, 0))] + state_specs + specs
    out_shape = (jax.ShapeDtypeStruct((b, ts, D_MODEL), F32),
                 jax.ShapeDtypeStruct(cache.shape, F32),
                 jax.ShapeDtypeStruct(s0.shape, F32))
    out_specs = [pl.BlockSpec((nb, ts, D_MODEL), lambda i: (i, 0, 0))] + state_specs
    n_in = 3 + len(ops)
    aliases = {n_in - len(prev) + k: 1 + k for k in range(len(prev))}
    return pl.pallas_call(
        functools.partial(_sample_kernel, nb=nb, ts=ts, final=final),
        grid=(b // nb,),
        in_specs=in_specs, out_specs=out_specs, out_shape=out_shape,
        scratch_shapes=[pltpu.VMEM((nb, HIST + ts + 2, D_CONV), F32)],
        input_output_aliases=aliases,
        compiler_params=pltpu.CompilerParams(
            dimension_semantics=("arbitrary",), vmem_limit_bytes=VMEM_LIMIT),
        name="sample_layer",
    )(x, cache, s0, *ops)


def _lower_tri(n):
    return jnp.asarray(np.tril(np.ones((n, n), np.float32)), BF16)


def _block_tri(n, blk):
    i = np.arange(n)
    m = (i[None, :] <= i[:, None]) & (i[None, :] // blk == i[:, None] // blk)
    return jnp.asarray(m.astype(np.float32), BF16)


def _block_ones(n, blk):
    i = np.arange(n)
    return jnp.asarray((i[None, :] // blk == i[:, None] // blk).astype(np.float32), BF16)


def _group_mean_matrix():
    gsz = D_CONV // CONV_GROUPS
    i = np.arange(D_CONV)
    m = (i[None, :] // gsz == i[:, None] // gsz).astype(np.float32) / gsz
    return jnp.asarray(m, BF16)


def _layer_weights(l, winb, wlrt, wpwb, woutb, norm_g, w_alpha, b_alpha, conv_w, conv_b, cn_g,
                   cn_b, b_pw, gla_g, final_g):
    row = lambda a: a.reshape(1, -1).astype(F32)
    wal = jnp.pad(w_alpha[l], ((0, LR_PAD - GATE_RANK), (0, 0))).astype(BF16)
    cwb = jnp.broadcast_to(conv_w[l][:, None, :], (CONV_W, SUBLANE, D_CONV)).astype(F32)
    return (row(norm_g[l]), winb, wlrt, wal, row(b_alpha[l]), cwb, row(conv_b[l]), row(cn_g[l]),
            row(cn_b[l]), _group_mean_matrix(), wpwb, row(b_pw[l]),
            row(gla_g[l]), woutb, row(final_g))


LAYER_STACKED = (1, 2, 10, 13)


def kernel(x_prompt, x_sample, cache_conv, state_gla, norm_g, w_in, w_alpha, b_alpha, conv_w,
           conv_b, cn_g, cn_b, w_pw, b_pw, gla_g, w_out, final_g):
    depth = w_in.shape[0]
    n_seq, n_dec = x_prompt.shape[0], x_sample.shape[0]
    state_in = state_gla.reshape(depth, n_dec, D_K, HEAD_V)
    winb, wlrt = _input_projection_bf16(w_in)
    wpwb, woutb = w_pw.astype(BF16), w_out.astype(BF16)
    hp, hs = x_prompt, x_sample
    prev_p, prev_s = (), ()
    for l in range(depth):
        lw = _layer_weights(l, winb, wlrt, wpwb, woutb, norm_g, w_alpha, b_alpha, conv_w, conv_b,
                            cn_g, cn_b, b_pw, gla_g, final_g)
        final = l == depth - 1
        hp, *prev_p = _prompt_layer(l, depth, hp, lw, prev_p, final)
        hs, *prev_s = _sample_layer(l, hs, cache_conv, state_in, lw, prev_s, final)
    conv_p, gla_p = prev_p
    conv_s, gla_s = prev_s
    return (hp, hs, conv_p, gla_p.reshape(depth, n_seq, HEADS, HEAD_K, HEAD_V),
            conv_s, gla_s.reshape(depth, n_dec, HEADS, HEAD_K, HEAD_V))
```

```python
import functools

import jax
import jax.numpy as jnp
import numpy as np
from jax import lax
from jax.experimental import pallas as pl
from jax.experimental.pallas import tpu as pltpu

F32 = jnp.float32
BF16 = jnp.bfloat16

D_MODEL = 1024
D_CONV = 512
CONV_GROUPS = 8
CONV_W = 31
HIST = CONV_W - 1
D_V = 512
HEADS = 4
D_K = 256
HEAD_K = 64
HEAD_V = 128
GATE_RANK = 16
GATE_NORM = 16.0
EPS = 1e-6

LANE = 128
SUBLANE = 8
HIST_PAD = 32
N_LANE_GROUPS = D_CONV // LANE
LR_PAD = LANE

OFF_A, OFF_GL, OFF_ZC = 0, 512, 1024
OFF_Q, OFF_K, OFF_V, OFF_ZG, OFF_LR = 1536, 1792, 2048, 2560, 3072
PROJ_COLS = OFF_LR - OFF_ZC

GLA_CHUNK = 128
PROMPT_TILE = 512
PROMPT_SEQS = 1
PROMPT_SEG = PROMPT_TILE // SUBLANE
UHALO0 = SUBLANE
UDATA0 = UHALO0 + HIST_PAD
USEG = UDATA0 + PROMPT_SEG
CSEG = PROMPT_SEG + SUBLANE
assert (USEG // SUBLANE) % 2 == 1 and (CSEG // SUBLANE) % 2 == 1
UTAIL = SUBLANE * USEG - HIST_PAD
SAMPLE_SEQS = 16
SLAB = SAMPLE_SEQS + SUBLANE
assert (SLAB // SUBLANE) % 2 == 1
CONV_UNROLL = 8
WPREP_ROWS = 512

VMEM_LIMIT = 56 * 1024 * 1024


def _dot(a, b):
    return jnp.dot(a, b, preferred_element_type=F32)


def _dot_nt(a, b):
    return lax.dot_general(a, b, (((1,), (1,)), ((), ())), preferred_element_type=F32)


def _sigmoid(x):
    return 1.0 / (1.0 + jnp.exp(-x))


def _silu(x):
    return x * _sigmoid(x)


def _log_sigmoid(x):
    return jnp.minimum(x, 0.0) - jnp.log(1.0 + jnp.exp(-jnp.abs(x)))


def _rms_rows(x, g):
    ms = jnp.mean(x * x, axis=-1, keepdims=True)
    return x * lax.rsqrt(ms + EPS) * g


def _split3(g):
    hi = g.astype(BF16)
    r1 = g - hi.astype(F32)
    mid = r1.astype(BF16)
    lo = (r1 - mid.astype(F32)).astype(BF16)
    return hi, mid, lo


def _sum_matmul(m, g):
    hi, mid, lo = _split3(g)
    return _dot(m, hi) + _dot(m, mid) + _dot(m, lo)


def _conv_post(cv, zc, cb_ref, cng_ref, cnb_ref, gm_ref, wpw_ref, bpw_ref):
    cv = cv + cb_ref[...]
    gm = gm_ref[...]
    mu = _dot(cv.astype(BF16), gm)
    d = cv - mu
    var = _dot((d * d).astype(BF16), gm)
    yn = d * lax.rsqrt(var + EPS) * cng_ref[...] + cnb_ref[...]
    pw = _dot(_silu(yn).astype(BF16), wpw_ref[...]) + bpw_ref[...]
    return pw * _silu(zc)


def _gate_log(hb, wlrt_ref, wal_ref, bal_ref):
    lr = _dot_nt(hb, wlrt_ref[...])
    z = _dot(lr.astype(BF16), wal_ref[...]) + bal_ref[...]
    return _log_sigmoid(z) * (1.0 / GATE_NORM)


def _gla_post(o, zg, gg_ref):
    parts = []
    for h in range(HEADS):
        oh = o[:, h * HEAD_V:(h + 1) * HEAD_V]
        ms = jnp.mean(oh * oh, axis=-1, keepdims=True)
        parts.append(oh * lax.rsqrt(ms + EPS) * gg_ref[...])
    return jnp.concatenate(parts, axis=-1) * _silu(zg)


def _finish(x, cat, wout_ref, fg_ref, final):
    y = x + _dot(cat, wout_ref[...])
    if final:
        y = _rms_rows(y, fg_ref[...])
    return y


def _wprep_kernel(wt_ref, lrt_ref, o_ref, olr_ref):
    o_ref[...] = wt_ref[...].T.astype(BF16)

    @pl.when(pl.program_id(1) == 0)
    def _():
        olr_ref[...] = jnp.zeros_like(olr_ref)
        olr_ref[0:GATE_RANK, :] = lrt_ref[...].astype(BF16)


def _input_projection_bf16(w_in):
    depth = w_in.shape[0]
    wt = jnp.swapaxes(w_in, 1, 2)
    return pl.pallas_call(
        _wprep_kernel,
        grid=(depth, OFF_LR // WPREP_ROWS),
        in_specs=[pl.BlockSpec((None, WPREP_ROWS, D_MODEL), lambda l, c: (l, c, 0)),
                  pl.BlockSpec((None, GATE_RANK, D_MODEL), lambda l, c: (l, OFF_LR // GATE_RANK, 0))],
        out_specs=[pl.BlockSpec((None, D_MODEL, WPREP_ROWS), lambda l, c: (l, 0, c)),
                   pl.BlockSpec((None, LR_PAD, D_MODEL), lambda l, c: (l, 0, 0))],
        out_shape=[jax.ShapeDtypeStruct((depth, D_MODEL, OFF_LR), BF16),
                   jax.ShapeDtypeStruct((depth, LR_PAD, D_MODEL), BF16)],
        compiler_params=pltpu.CompilerParams(dimension_semantics=("arbitrary", "arbitrary")),
        name="weight_layout",
    )(wt, wt)


def _prompt_kernel(x_ref, ng_ref, winb, wlrt_ref, wal_ref, bal_ref, cwb_ref, cb_ref, cng_ref,
                   cnb_ref, gm_ref, wpwb, bpw_ref, gg_ref, woutb, fg_ref, ltri_ref, *rest,
                   ns, tt, final):
    (y_ref, buf_ref, sout_ref, ubuf, cvs, proj, os_, cat, sst) = rest[-9:]
    t = pl.program_id(1)
    seg = tt // SUBLANE
    c = GLA_CHUNK
    half = c // 2

    @pl.when(t == 0)
    def _():
        ubuf[:, :, UTAIL:UTAIL + HIST_PAD, :] = jnp.zeros((ns, N_LANE_GROUPS, HIST_PAD, LANE), F32)
        sst[...] = jnp.zeros_like(sst)

    x = x_ref[...].reshape(ns * tt, D_MODEL)
    hb = _rms_rows(x, ng_ref[...]).astype(BF16)

    a = _dot(hb, winb[:, OFF_A:OFF_A + D_CONV])
    gl = _dot(hb, winb[:, OFF_GL:OFF_GL + D_CONV])
    u = a * _sigmoid(gl)
    for s in range(ns):
        for g in range(N_LANE_GROUPS):
            prev_tail = ubuf[s, g, UTAIL:UTAIL + HIST_PAD, :]
            ext = jnp.concatenate([prev_tail, u[s * tt:(s + 1) * tt, g * LANE:(g + 1) * LANE]], axis=0)
            for j in range(SUBLANE):
                ubuf[s, g, j * USEG + UHALO0:(j + 1) * USEG, :] = ext[j * seg:(j + 1) * seg + HIST_PAD, :]
    gk = _gate_log(hb, wlrt_ref, wal_ref, bal_ref)

    def project(lo, hi):
        proj[:, lo - OFF_ZC:hi - OFF_ZC] = _dot(hb, winb[:, lo:hi])

    project(OFF_Q, OFF_ZG)

    base = UDATA0 - HIST

    def conv_lane_group(g, s):
        w = [cwb_ref[k, :, g * LANE:(g + 1) * LANE] for k in range(CONV_W)]
        for i0 in range(0, seg, CONV_UNROLL):
            accs = [None] * CONV_UNROLL
            for m in range(CONV_UNROLL + CONV_W - 1):
                um = ubuf[s, g, pl.ds(i0 + base + m, SUBLANE, stride=USEG), :]
                for ii in range(CONV_UNROLL):
                    k = m - ii
                    if 0 <= k < CONV_W:
                        accs[ii] = um * w[k] if accs[ii] is None else accs[ii] + um * w[k]
            for ii in range(CONV_UNROLL):
                cvs[s, g, pl.ds(i0 + ii, SUBLANE, stride=CSEG), :] = accs[ii]

    row = lax.broadcasted_iota(jnp.int32, (c, c), 0)
    col = lax.broadcasted_iota(jnp.int32, (c, c), 1)
    causal = col <= row
    head_lo = col < HEAD_K
    ltri = ltri_ref[...]

    def attention_block(r0, s_old):
        rows = slice(r0, r0 + c)
        gcum = _sum_matmul(ltri, gk[rows, :])
        q = proj[rows, OFF_Q - OFF_ZC:OFF_K - OFF_ZC] * (HEAD_K ** -0.5)
        k = proj[rows, OFF_K - OFF_ZC:OFF_V - OFF_ZC]
        v = proj[rows, OFF_V - OFF_ZC:OFF_ZG - OFF_ZC].astype(BF16)
        mid = gcum[half - 1:half, :]
        xq = q * jnp.exp(gcum - mid)
        yk = (k * jnp.exp(mid - gcum)).astype(BF16)
        qg = q * jnp.exp(gcum)
        gt = gcum.T
        glast = gt[:, c - 1:c]
        kdt = (k.T * jnp.exp(glast - gt)).astype(BF16)
        dcol = jnp.exp(glast)
        s_bf = s_old.astype(BF16)
        outs = []
        s_new = []
        for h in range(HEADS):
            j, hi = divmod(h, 2)
            pair = slice(j * LANE, (j + 1) * LANE)
            hm = head_lo if hi == 0 else jnp.logical_not(head_lo)
            xm = jnp.where(hm, xq[:, pair], 0.0).astype(BF16)
            qm = jnp.where(hm, qg[:, pair], 0.0).astype(BF16)
            att = jnp.where(causal, _dot_nt(xm, yk[:, pair]), 0.0).astype(BF16)
            vh = v[:, h * HEAD_V:(h + 1) * HEAD_V]
            outs.append(_dot(att, vh) + _dot(qm, s_bf[pair, :]))
            hr = slice(h * HEAD_K, (h + 1) * HEAD_K)
            s_new.append(dcol[hr, :] * s_old[hr, :] + _dot(kdt[hr, :], vh))
        os_[rows, :] = jnp.concatenate(outs, axis=-1)
        return jnp.concatenate(s_new, axis=0)

    fill_cols = D_CONV // 2
    fillers = [(lo, lo + fill_cols) for lo in (OFF_ZC, OFF_ZC + fill_cols, OFF_ZG, OFF_ZG + fill_cols)]
    conv_items = [(g, s) for s in range(ns) for g in range(N_LANE_GROUPS)]
    blocks = [(s, i) for s in range(ns) for i in range(tt // c)]
    states = [sst[s] for s in range(ns)]
    for n in range(max(len(conv_items), len(blocks), len(fillers))):
        if n < len(fillers):
            project(*fillers[n])
        if n < len(conv_items):
            conv_lane_group(*conv_items[n])
        if n < len(blocks):
            s, i = blocks[n]
            states[s] = attention_block(s * tt + i * c, states[s])
    for s in range(ns):
        sst[s] = states[s]

    cv = jnp.concatenate(
        [jnp.concatenate(
            [jnp.concatenate([cvs[s, g, j * CSEG:j * CSEG + seg, :] for j in range(SUBLANE)], axis=0)
             for g in range(N_LANE_GROUPS)], axis=-1) for s in range(ns)], axis=0)
    zc = proj[:, 0:OFF_Q - OFF_ZC]
    cvo = _conv_post(cv, zc, cb_ref, cng_ref, cnb_ref, gm_ref, wpwb, bpw_ref)
    cat[:, 0:D_CONV] = cvo.astype(BF16)

    zg = proj[:, OFF_ZG - OFF_ZC:OFF_LR - OFF_ZC]
    cat[:, D_CONV:] = _gla_post(os_[...], zg, gg_ref).astype(BF16)
    y_ref[...] = _finish(x, cat[...], woutb, fg_ref, final).reshape(ns, tt, D_MODEL)

    @pl.when(t == pl.num_programs(1) - 1)
    def _():
        for s in range(ns):
            buf_ref[s] = jnp.concatenate(
                [ubuf[s, g, UTAIL + HIST_PAD - HIST:UTAIL + HIST_PAD, :]
                 for g in range(N_LANE_GROUPS)], axis=-1)
        sout_ref[...] = sst[...]


def _const_spec(a):
    return pl.BlockSpec(a.shape, lambda *_: (0,) * a.ndim, pipeline_mode=pl.Buffered(1))


def _layer_spec(a, l):
    return pl.BlockSpec((None,) + a.shape[1:], lambda *_: (l,) + (0,) * (a.ndim - 1),
                        pipeline_mode=pl.Buffered(1))


def _operands(l, lw, masks, prev):
    ops = list(lw) + list(masks) + list(prev)
    specs = [_layer_spec(a, l) if i in LAYER_STACKED else _const_spec(a) for i, a in enumerate(lw)]
    specs += [_const_spec(m) for m in masks]
    specs += [pl.BlockSpec(memory_space=pl.ANY) for _ in prev]
    return ops, specs


def _prompt_layer(l, depth, x, lw, prev, final):
    b, t, _ = x.shape
    ns, tt = PROMPT_SEQS, PROMPT_TILE
    assert b % ns == 0 and t % tt == 0 and tt % GLA_CHUNK == 0
    assert tt % (SUBLANE * CONV_UNROLL) == 0 and t >= HIST_PAD
    ops, specs = _operands(l, lw, (_lower_tri(GLA_CHUNK),), prev)
    in_specs = [pl.BlockSpec((ns, tt, D_MODEL), lambda i, j: (i, j, 0))] + specs
    out_shape = (jax.ShapeDtypeStruct((b, t, D_MODEL), F32),
                 jax.ShapeDtypeStruct((depth, b, HIST, D_CONV), F32),
                 jax.ShapeDtypeStruct((depth, b, D_K, HEAD_V), F32))
    out_specs = (pl.BlockSpec((ns, tt, D_MODEL), lambda i, j: (i, j, 0)),
                 pl.BlockSpec((None, ns, HIST, D_CONV), lambda i, j: (l, i, 0, 0)),
                 pl.BlockSpec((None, ns, D_K, HEAD_V), lambda i, j: (l, i, 0, 0)))
    n_in = 1 + len(ops)
    aliases = {n_in - len(prev) + k: 1 + k for k in range(len(prev))}
    scratch = [pltpu.VMEM((ns, N_LANE_GROUPS, SUBLANE * USEG, LANE), F32),
               pltpu.VMEM((ns, N_LANE_GROUPS, SUBLANE * CSEG, LANE), F32),
               pltpu.VMEM((ns * tt, PROJ_COLS), F32),
               pltpu.VMEM((ns * tt, D_V), F32),
               pltpu.VMEM((ns * tt, D_MODEL), BF16),
               pltpu.VMEM((ns, D_K, HEAD_V), F32)]
    return pl.pallas_call(
        functools.partial(_prompt_kernel, ns=ns, tt=tt, final=final),
        grid=(b // ns, t // tt),
        in_specs=in_specs, out_specs=out_specs, out_shape=out_shape,
        scratch_shapes=scratch,
        input_output_aliases=aliases,
        compiler_params=pltpu.CompilerParams(
            dimension_semantics=("arbitrary", "arbitrary"), vmem_limit_bytes=VMEM_LIMIT),
        name="prompt_layer",
    )(x, *ops)


def _sample_kernel(x_ref, cache_ref, s0_ref, ng_ref, winb, wlrt_ref, wal_ref, bal_ref, cwb_ref,
                   cb_ref, cng_ref, cnb_ref, gm_ref, wpwb, bpw_ref, gg_ref, woutb, fg_ref,
                   ltri_ref, ball_ref, *rest, nb, ts, final):
    (y_ref, buf_ref, sout_ref, full, cvt) = rest[-5:]
    r = nb * ts
    x = x_ref[...].reshape(r, D_MODEL)
    hb = _rms_rows(x, ng_ref[...]).astype(BF16)

    a = _dot(hb, winb[:, OFF_A:OFF_A + D_CONV])
    gl = _dot(hb, winb[:, OFF_GL:OFF_GL + D_CONV])
    u = a * _sigmoid(gl)
    for g in range(N_LANE_GROUPS):
        lanes = slice(g * LANE, (g + 1) * LANE)
        for t in range(HIST):
            full[g, t * SLAB:t * SLAB + nb, :] = cache_ref[t, :, lanes]
        for b in range(nb):
            full[g, pl.ds(HIST * SLAB + b, ts, stride=SLAB), :] = u[b * ts:(b + 1) * ts, lanes]
    for g in range(N_LANE_GROUPS):
        w = [cwb_ref[k, :, g * LANE:(g + 1) * LANE] for k in range(CONV_W)]
        for t in range(ts):
            acc = None
            for k in range(CONV_W):
                slab = full[g, (t + k) * SLAB:(t + k) * SLAB + nb, :].reshape(nb // SUBLANE, SUBLANE, LANE)
                acc = slab * w[k] if acc is None else acc + slab * w[k]
            cvt[g, t * SLAB:t * SLAB + nb, :] = acc.reshape(nb, LANE)
    for t in range(HIST):
        buf_ref[t] = jnp.concatenate(
            [full[g, (ts + t) * SLAB:(ts + t) * SLAB + nb, :] for g in range(N_LANE_GROUPS)], axis=-1)
    cv = jnp.concatenate(
        [jnp.concatenate([cvt[g, pl.ds(b, ts, stride=SLAB), :] for b in range(nb)], axis=0)
         for g in range(N_LANE_GROUPS)], axis=-1)
    zc = _dot(hb, winb[:, OFF_ZC:OFF_ZC + D_CONV])
    cvo = _conv_post(cv, zc, cb_ref, cng_ref, cnb_ref, gm_ref, wpwb, bpw_ref)

    q = _dot(hb, winb[:, OFF_Q:OFF_Q + D_K]) * (HEAD_K ** -0.5)
    k = _dot(hb, winb[:, OFF_K:OFF_K + D_K])
    v = _dot(hb, winb[:, OFF_V:OFF_V + D_V]).astype(BF16)
    g = _gate_log(hb, wlrt_ref, wal_ref, bal_ref)
    gcum = _sum_matmul(ltri_ref[...], g)
    gtot = _sum_matmul(ball_ref[...], g)
    xq = q * jnp.exp(gcum)
    yk = (k * jnp.exp(-gcum)).astype(BF16)
    kdt = (k * jnp.exp(gtot - gcum)).T.astype(BF16)
    gt = g.T

    row = lax.broadcasted_iota(jnp.int32, (r, r), 0)
    col = lax.broadcasted_iota(jnp.int32, (r, r), 1)
    causal = jnp.logical_and(col <= row, (col // ts) == (row // ts))
    head_lo = lax.broadcasted_iota(jnp.int32, (r, LANE), 1) < HEAD_K
    seq_sel = (lax.broadcasted_iota(jnp.int32, (nb, 1, r), 2) // ts
               == lax.broadcasted_iota(jnp.int32, (nb, 1, r), 0))
    blk = (lax.broadcasted_iota(jnp.int32, (r, nb * LANE), 1) // LANE
           == lax.broadcasted_iota(jnp.int32, (r, nb * LANE), 0) // ts)

    s0 = s0_ref[...]
    s0_bf = s0.astype(BF16)
    gsum = jnp.sum(jnp.where(seq_sel, gt[None, :, :], 0.0), axis=-1, keepdims=True)
    s_new = jnp.exp(gsum) * s0

    outs = []
    kvs = []
    for h in range(HEADS):
        j, hi = divmod(h, 2)
        pair = slice(j * LANE, (j + 1) * LANE)
        hm = head_lo if hi == 0 else jnp.logical_not(head_lo)
        xm = jnp.where(hm, xq[:, pair], 0.0).astype(BF16)
        att = jnp.where(causal, _dot_nt(xm, yk[:, pair]), 0.0).astype(BF16)
        vh = v[:, h * HEAD_V:(h + 1) * HEAD_V]
        xblk = jnp.where(blk, jnp.concatenate([xm] * nb, axis=-1), jnp.zeros((), BF16))
        s_pair = s0_bf[:, pair, :].reshape(nb * LANE, HEAD_V)
        outs.append(_dot(att, vh) + _dot(xblk, s_pair))
        kh = kdt[h * HEAD_K:(h + 1) * HEAD_K, :]
        kblk = jnp.where(seq_sel, kh[None, :, :], jnp.zeros((), BF16)).reshape(nb * HEAD_K, r)
        kvs.append(_dot(kblk, vh).reshape(nb, HEAD_K, HEAD_V))
    sout_ref[...] = s_new + jnp.concatenate(kvs, axis=1)

    o = jnp.concatenate(outs, axis=-1)
    zg = _dot(hb, winb[:, OFF_ZG:OFF_ZG + D_V])
    cat = jnp.concatenate([cvo.astype(BF16), _gla_post(o, zg, gg_ref).astype(BF16)], axis=-1)
    y_ref[...] = _finish(x, cat, woutb, fg_ref, final).reshape(nb, ts, D_MODEL)


def _sample_layer(l, x, cache, s0, lw, prev, final):
    b, ts, _ = x.shape
    nb = SAMPLE_SEQS
    assert b % nb == 0 and ts == SUBLANE and nb * ts == LANE
    r = nb * ts
    ops, specs = _operands(l, lw, (_block_tri(r, ts), _block_ones(r, ts)), prev)
    state_specs = [pl.BlockSpec((None, HIST, nb, D_CONV), lambda i: (l, 0, i, 0)),
                   pl.BlockSpec((None, nb, D_K, HEAD_V), lambda i: (l, i, 0, 0))]
    in_specs = [pl.BlockSpec((nb, ts, D_MODEL), lambda i: (i, 0, 0))] + state_specs + specs
    out_shape = (jax.ShapeDtypeStruct((b, ts, D_MODEL), F32),
                 jax.ShapeDtypeStruct(cache.shape, F32),
                 jax.ShapeDtypeStruct(s0.shape, F32))
    out_specs = [pl.BlockSpec((nb, ts, D_MODEL), lambda i: (i, 0, 0))] + state_specs
    n_in = 3 + len(ops)
    aliases = {n_in - len(prev) + k: 1 + k for k in range(len(prev))}
    return pl.pallas_call(
        functools.partial(_sample_kernel, nb=nb, ts=ts, final=final),
        grid=(b // nb,),
        in_specs=in_specs, out_specs=out_specs, out_shape=out_shape,
        scratch_shapes=[pltpu.VMEM((N_LANE_GROUPS, (HIST + ts) * SLAB, LANE), F32),
                        pltpu.VMEM((N_LANE_GROUPS, ts * SLAB, LANE), F32)],
        input_output_aliases=aliases,
        compiler_params=pltpu.CompilerParams(
            dimension_semantics=("arbitrary",), vmem_limit_bytes=VMEM_LIMIT),
        name="sample_layer",
    )(x, cache, s0, *ops)


def _lower_tri(n):
    return jnp.asarray(np.tril(np.ones((n, n), np.float32)), BF16)


def _block_tri(n, blk):
    i = np.arange(n)
    m = (i[None, :] <= i[:, None]) & (i[None, :] // blk == i[:, None] // blk)
    return jnp.asarray(m.astype(np.float32), BF16)


def _block_ones(n, blk):
    i = np.arange(n)
    return jnp.asarray((i[None, :] // blk == i[:, None] // blk).astype(np.float32), BF16)


def _group_mean_matrix():
    gsz = D_CONV // CONV_GROUPS
    i = np.arange(D_CONV)
    m = (i[None, :] // gsz == i[:, None] // gsz).astype(np.float32) / gsz
    return jnp.asarray(m, BF16)


def _layer_weights(l, winb, wlrt, wpwb, woutb, norm_g, w_alpha, b_alpha, conv_w, conv_b, cn_g,
                   cn_b, b_pw, gla_g, final_g):
    row = lambda a: a.reshape(1, -1).astype(F32)
    wal = jnp.pad(w_alpha[l], ((0, LR_PAD - GATE_RANK), (0, 0))).astype(BF16)
    cwb = jnp.broadcast_to(conv_w[l][:, None, :], (CONV_W, SUBLANE, D_CONV)).astype(F32)
    return (row(norm_g[l]), winb, wlrt, wal, row(b_alpha[l]), cwb, row(conv_b[l]), row(cn_g[l]),
            row(cn_b[l]), _group_mean_matrix(), wpwb, row(b_pw[l]),
            row(gla_g[l]), woutb, row(final_g))


LAYER_STACKED = (1, 2, 10, 13)


def kernel(x_prompt, x_sample, cache_conv, state_gla, norm_g, w_in, w_alpha, b_alpha, conv_w,
           conv_b, cn_g, cn_b, w_pw, b_pw, gla_g, w_out, final_g):
    depth = w_in.shape[0]
    n_seq, n_dec = x_prompt.shape[0], x_sample.shape[0]
    state_in = state_gla.reshape(depth, n_dec, D_K, HEAD_V)
    cache_in = jnp.swapaxes(cache_conv, 1, 2)
    winb, wlrt = _input_projection_bf16(w_in)
    wpwb, woutb = w_pw.astype(BF16), w_out.astype(BF16)
    hp, hs = x_prompt, x_sample
    prev_p, prev_s = (), ()
    for l in range(depth):
        lw = _layer_weights(l, winb, wlrt, wpwb, woutb, norm_g, w_alpha, b_alpha, conv_w, conv_b,
                            cn_g, cn_b, b_pw, gla_g, final_g)
        final = l == depth - 1
        hp, *prev_p = _prompt_layer(l, depth, hp, lw, prev_p, final)
        hs, *prev_s = _sample_layer(l, hs, cache_in, state_in, lw, prev_s, final)
    conv_p, gla_p = prev_p
    conv_s, gla_s = prev_s
    return (hp, hs, conv_p, gla_p.reshape(depth, n_seq, HEADS, HEAD_K, HEAD_V),
            jnp.swapaxes(conv_s, 1, 2), gla_s.reshape(depth, n_dec, HEADS, HEAD_K, HEAD_V))
```

```python
import functools

import jax
import jax.numpy as jnp
import numpy as np
from jax import lax
from jax.experimental import pallas as pl
from jax.experimental.pallas import tpu as pltpu

F32 = jnp.float32
BF16 = jnp.bfloat16

D_MODEL = 1024
D_CONV = 512
CONV_GROUPS = 8
CONV_W = 31
HIST = CONV_W - 1
D_V = 512
HEADS = 4
D_K = 256
HEAD_K = 64
HEAD_V = 128
GATE_RANK = 16
GATE_NORM = 16.0
EPS = 1e-6

LANE = 128
SUBLANE = 8
HIST_PAD = 32
N_LANE_GROUPS = D_CONV // LANE
LR_PAD = LANE

OFF_A, OFF_GL, OFF_ZC = 0, 512, 1024
OFF_Q, OFF_K, OFF_V, OFF_ZG, OFF_LR = 1536, 1792, 2048, 2560, 3072
PROJ_COLS = OFF_LR - OFF_ZC

GLA_CHUNK = 128
PROMPT_TILE = 512
PROMPT_SEQS = 1
PROMPT_SEG = PROMPT_TILE // SUBLANE
UHALO0 = SUBLANE
UDATA0 = UHALO0 + HIST_PAD
USEG = UDATA0 + PROMPT_SEG
CSEG = PROMPT_SEG + SUBLANE
assert (USEG // SUBLANE) % 2 == 1 and (CSEG // SUBLANE) % 2 == 1
UTAIL = SUBLANE * USEG - HIST_PAD
SAMPLE_SEQS = 16
SLAB = SAMPLE_SEQS + SUBLANE
assert (SLAB // SUBLANE) % 2 == 1
CONV_UNROLL = 8
WPREP_ROWS = 512

VMEM_LIMIT = 56 * 1024 * 1024


def _dot(a, b):
    return jnp.dot(a, b, preferred_element_type=F32)


def _dot_nt(a, b):
    return lax.dot_general(a, b, (((1,), (1,)), ((), ())), preferred_element_type=F32)


def _sigmoid(x):
    return 1.0 / (1.0 + jnp.exp(-x))


def _silu(x):
    return x * _sigmoid(x)


def _log_sigmoid(x):
    return jnp.minimum(x, 0.0) - jnp.log(1.0 + jnp.exp(-jnp.abs(x)))


def _rms_rows(x, g):
    ms = jnp.mean(x * x, axis=-1, keepdims=True)
    return x * lax.rsqrt(ms + EPS) * g


def _split3(g):
    hi = g.astype(BF16)
    r1 = g - hi.astype(F32)
    mid = r1.astype(BF16)
    lo = (r1 - mid.astype(F32)).astype(BF16)
    return hi, mid, lo


def _sum_matmul(m, g):
    hi, mid, lo = _split3(g)
    return _dot(m, hi) + _dot(m, mid) + _dot(m, lo)


def _conv_post(cv, zc, cb_ref, cng_ref, cnb_ref, gm_ref, wpw_ref, bpw_ref):
    cv = cv + cb_ref[...]
    gm = gm_ref[...]
    mu = _dot(cv.astype(BF16), gm)
    d = cv - mu
    var = _dot((d * d).astype(BF16), gm)
    yn = d * lax.rsqrt(var + EPS) * cng_ref[...] + cnb_ref[...]
    pw = _dot(_silu(yn).astype(BF16), wpw_ref[...]) + bpw_ref[...]
    return pw * _silu(zc)


def _gate_log(hb, wlrt_ref, wal_ref, bal_ref):
    lr = _dot_nt(hb, wlrt_ref[...])
    z = _dot(lr.astype(BF16), wal_ref[...]) + bal_ref[...]
    return _log_sigmoid(z) * (1.0 / GATE_NORM)


def _gla_post(o, zg, gg_ref):
    parts = []
    for h in range(HEADS):
        oh = o[:, h * HEAD_V:(h + 1) * HEAD_V]
        ms = jnp.mean(oh * oh, axis=-1, keepdims=True)
        parts.append(oh * lax.rsqrt(ms + EPS) * gg_ref[...])
    return jnp.concatenate(parts, axis=-1) * _silu(zg)


def _finish(x, cat, wout_ref, fg_ref, final):
    y = x + _dot(cat, wout_ref[...])
    if final:
        y = _rms_rows(y, fg_ref[...])
    return y


def _wprep_kernel(wt_ref, lrt_ref, o_ref, olr_ref):
    o_ref[...] = wt_ref[...].T.astype(BF16)

    @pl.when(pl.program_id(1) == 0)
    def _():
        olr_ref[...] = jnp.zeros_like(olr_ref)
        olr_ref[0:GATE_RANK, :] = lrt_ref[...].astype(BF16)


def _input_projection_bf16(w_in):
    depth = w_in.shape[0]
    wt = jnp.swapaxes(w_in, 1, 2)
    return pl.pallas_call(
        _wprep_kernel,
        grid=(depth, OFF_LR // WPREP_ROWS),
        in_specs=[pl.BlockSpec((None, WPREP_ROWS, D_MODEL), lambda l, c: (l, c, 0)),
                  pl.BlockSpec((None, GATE_RANK, D_MODEL), lambda l, c: (l, OFF_LR // GATE_RANK, 0))],
        out_specs=[pl.BlockSpec((None, D_MODEL, WPREP_ROWS), lambda l, c: (l, 0, c)),
                   pl.BlockSpec((None, LR_PAD, D_MODEL), lambda l, c: (l, 0, 0))],
        out_shape=[jax.ShapeDtypeStruct((depth, D_MODEL, OFF_LR), BF16),
                   jax.ShapeDtypeStruct((depth, LR_PAD, D_MODEL), BF16)],
        compiler_params=pltpu.CompilerParams(dimension_semantics=("arbitrary", "arbitrary")),
        name="weight_layout",
    )(wt, wt)


def _prompt_kernel(x_ref, ng_ref, winb, wlrt_ref, wal_ref, bal_ref, cwb_ref, cb_ref, cng_ref,
                   cnb_ref, gm_ref, wpwb, bpw_ref, gg_ref, woutb, fg_ref, ltri_ref, *rest,
                   ns, tt, final):
    (y_ref, buf_ref, sout_ref, ubuf, cvs, proj, os_, cat, sst) = rest[-9:]
    t = pl.program_id(1)
    seg = tt // SUBLANE
    c = GLA_CHUNK
    half = c // 2

    @pl.when(t == 0)
    def _():
        ubuf[:, :, UTAIL:UTAIL + HIST_PAD, :] = jnp.zeros((ns, N_LANE_GROUPS, HIST_PAD, LANE), F32)
        sst[...] = jnp.zeros_like(sst)

    x = x_ref[...].reshape(ns * tt, D_MODEL)
    hb = _rms_rows(x, ng_ref[...]).astype(BF16)

    a = _dot(hb, winb[:, OFF_A:OFF_A + D_CONV])
    gl = _dot(hb, winb[:, OFF_GL:OFF_GL + D_CONV])
    u = a * _sigmoid(gl)
    for s in range(ns):
        for g in range(N_LANE_GROUPS):
            prev_tail = ubuf[s, g, UTAIL:UTAIL + HIST_PAD, :]
            ext = jnp.concatenate([prev_tail, u[s * tt:(s + 1) * tt, g * LANE:(g + 1) * LANE]], axis=0)
            for j in range(SUBLANE):
                ubuf[s, g, j * USEG + UHALO0:(j + 1) * USEG, :] = ext[j * seg:(j + 1) * seg + HIST_PAD, :]
    gk = _gate_log(hb, wlrt_ref, wal_ref, bal_ref)

    def project(lo, hi):
        proj[:, lo - OFF_ZC:hi - OFF_ZC] = _dot(hb, winb[:, lo:hi])

    project(OFF_Q, OFF_ZG)

    base = UDATA0 - HIST

    def conv_lane_group(g, s):
        w = [cwb_ref[k, :, g * LANE:(g + 1) * LANE] for k in range(CONV_W)]
        for i0 in range(0, seg, CONV_UNROLL):
            accs = [None] * CONV_UNROLL
            for m in range(CONV_UNROLL + CONV_W - 1):
                um = ubuf[s, g, pl.ds(i0 + base + m, SUBLANE, stride=USEG), :]
                for ii in range(CONV_UNROLL):
                    k = m - ii
                    if 0 <= k < CONV_W:
                        accs[ii] = um * w[k] if accs[ii] is None else accs[ii] + um * w[k]
            for ii in range(CONV_UNROLL):
                cvs[s, g, pl.ds(i0 + ii, SUBLANE, stride=CSEG), :] = accs[ii]

    row = lax.broadcasted_iota(jnp.int32, (c, c), 0)
    col = lax.broadcasted_iota(jnp.int32, (c, c), 1)
    causal2 = jnp.concatenate([col <= row] * 2, axis=1)
    ltri = ltri_ref[...]

    def head_diag(top, bot):
        return jnp.concatenate(
            [jnp.concatenate([top, jnp.zeros_like(top)], axis=1),
             jnp.concatenate([jnp.zeros_like(bot), bot], axis=1)], axis=0)

    def attention_block(r0, s_old):
        rows = slice(r0, r0 + c)
        gcum = _sum_matmul(ltri, gk[rows, :])
        q = proj[rows, OFF_Q - OFF_ZC:OFF_K - OFF_ZC] * (HEAD_K ** -0.5)
        k = proj[rows, OFF_K - OFF_ZC:OFF_V - OFF_ZC]
        v = proj[rows, OFF_V - OFF_ZC:OFF_ZG - OFF_ZC].astype(BF16)
        mid = gcum[half - 1:half, :]
        xq = (q * jnp.exp(gcum - mid)).astype(BF16)
        qg = (q * jnp.exp(gcum)).astype(BF16)
        gt = gcum.T
        kt = k.T
        midc = gt[:, half - 1:half]
        glast = gt[:, c - 1:c]
        ykt = (kt * jnp.exp(midc - gt)).astype(BF16)
        kdt = (kt * jnp.exp(glast - gt)).astype(BF16)
        dcol = jnp.exp(glast)
        s_bf = s_old.astype(BF16)
        outs = []
        s_new = []
        for j in range(HEADS // 2):
            pair = slice(j * LANE, (j + 1) * LANE)
            lo = slice(j * LANE, j * LANE + HEAD_K)
            hi = slice(j * LANE + HEAD_K, (j + 1) * LANE)
            v2 = v[:, 2 * j * HEAD_V:(2 * j + 2) * HEAD_V]
            att = _dot(xq[:, pair], head_diag(ykt[lo, :], ykt[hi, :]))
            att = jnp.where(causal2, att, 0.0).astype(BF16)
            outs.append(_dot(att, head_diag(v2[:, 0:HEAD_V], v2[:, HEAD_V:]))
                        + _dot(qg[:, pair], head_diag(s_bf[lo, :], s_bf[hi, :])))
            kv = _dot(kdt[pair, :], v2)
            s_new.append(dcol[lo, :] * s_old[lo, :] + kv[0:HEAD_K, 0:HEAD_V])
            s_new.append(dcol[hi, :] * s_old[hi, :] + kv[HEAD_K:, HEAD_V:])
        os_[rows, :] = jnp.concatenate(outs, axis=-1)
        return jnp.concatenate(s_new, axis=0)

    fill_cols = D_CONV // 2
    fillers = [(lo, lo + fill_cols) for lo in (OFF_ZC, OFF_ZC + fill_cols, OFF_ZG, OFF_ZG + fill_cols)]
    conv_items = [(g, s) for s in range(ns) for g in range(N_LANE_GROUPS)]
    blocks = [(s, i) for s in range(ns) for i in range(tt // c)]
    states = [sst[s] for s in range(ns)]
    for n in range(max(len(conv_items), len(blocks), len(fillers))):
        if n < len(fillers):
            project(*fillers[n])
        if n < len(conv_items):
            conv_lane_group(*conv_items[n])
        if n < len(blocks):
            s, i = blocks[n]
            states[s] = attention_block(s * tt + i * c, states[s])
    for s in range(ns):
        sst[s] = states[s]

    cv = jnp.concatenate(
        [jnp.concatenate(
            [jnp.concatenate([cvs[s, g, j * CSEG:j * CSEG + seg, :] for j in range(SUBLANE)], axis=0)
             for g in range(N_LANE_GROUPS)], axis=-1) for s in range(ns)], axis=0)
    zc = proj[:, 0:OFF_Q - OFF_ZC]
    cvo = _conv_post(cv, zc, cb_ref, cng_ref, cnb_ref, gm_ref, wpwb, bpw_ref)
    cat[:, 0:D_CONV] = cvo.astype(BF16)

    zg = proj[:, OFF_ZG - OFF_ZC:OFF_LR - OFF_ZC]
    cat[:, D_CONV:] = _gla_post(os_[...], zg, gg_ref).astype(BF16)
    y_ref[...] = _finish(x, cat[...], woutb, fg_ref, final).reshape(ns, tt, D_MODEL)

    @pl.when(t == pl.num_programs(1) - 1)
    def _():
        for s in range(ns):
            buf_ref[s] = jnp.concatenate(
                [ubuf[s, g, UTAIL + HIST_PAD - HIST:UTAIL + HIST_PAD, :]
                 for g in range(N_LANE_GROUPS)], axis=-1)
        sout_ref[...] = sst[...]


def _const_spec(a):
    return pl.BlockSpec(a.shape, lambda *_: (0,) * a.ndim, pipeline_mode=pl.Buffered(1))


def _layer_spec(a, l):
    return pl.BlockSpec((None,) + a.shape[1:], lambda *_: (l,) + (0,) * (a.ndim - 1),
                        pipeline_mode=pl.Buffered(1))


def _operands(l, lw, masks, prev):
    ops = list(lw) + list(masks) + list(prev)
    specs = [_layer_spec(a, l) if i in LAYER_STACKED else _const_spec(a) for i, a in enumerate(lw)]
    specs += [_const_spec(m) for m in masks]
    specs += [pl.BlockSpec(memory_space=pl.ANY) for _ in prev]
    return ops, specs


def _prompt_layer(l, depth, x, lw, prev, final):
    b, t, _ = x.shape
    ns, tt = PROMPT_SEQS, PROMPT_TILE
    assert b % ns == 0 and t % tt == 0 and tt % GLA_CHUNK == 0
    assert tt % (SUBLANE * CONV_UNROLL) == 0 and t >= HIST_PAD
    ops, specs = _operands(l, lw, (_lower_tri(GLA_CHUNK),), prev)
    in_specs = [pl.BlockSpec((ns, tt, D_MODEL), lambda i, j: (i, j, 0))] + specs
    out_shape = (jax.ShapeDtypeStruct((b, t, D_MODEL), F32),
                 jax.ShapeDtypeStruct((depth, b, HIST, D_CONV), F32),
                 jax.ShapeDtypeStruct((depth, b, D_K, HEAD_V), F32))
    out_specs = (pl.BlockSpec((ns, tt, D_MODEL), lambda i, j: (i, j, 0)),
                 pl.BlockSpec((None, ns, HIST, D_CONV), lambda i, j: (l, i, 0, 0)),
                 pl.BlockSpec((None, ns, D_K, HEAD_V), lambda i, j: (l, i, 0, 0)))
    n_in = 1 + len(ops)
    aliases = {n_in - len(prev) + k: 1 + k for k in range(len(prev))}
    scratch = [pltpu.VMEM((ns, N_LANE_GROUPS, SUBLANE * USEG, LANE), F32),
               pltpu.VMEM((ns, N_LANE_GROUPS, SUBLANE * CSEG, LANE), F32),
               pltpu.VMEM((ns * tt, PROJ_COLS), F32),
               pltpu.VMEM((ns * tt, D_V), F32),
               pltpu.VMEM((ns * tt, D_MODEL), BF16),
               pltpu.VMEM((ns, D_K, HEAD_V), F32)]
    return pl.pallas_call(
        functools.partial(_prompt_kernel, ns=ns, tt=tt, final=final),
        grid=(b // ns, t // tt),
        in_specs=in_specs, out_specs=out_specs, out_shape=out_shape,
        scratch_shapes=scratch,
        input_output_aliases=aliases,
        compiler_params=pltpu.CompilerParams(
            dimension_semantics=("arbitrary", "arbitrary"), vmem_limit_bytes=VMEM_LIMIT),
        name="prompt_layer",
    )(x, *ops)


def _sample_kernel(x_ref, cache_ref, s0_ref, ng_ref, winb, wlrt_ref, wal_ref, bal_ref, cwb_ref,
                   cb_ref, cng_ref, cnb_ref, gm_ref, wpwb, bpw_ref, gg_ref, woutb, fg_ref,
                   ltri_ref, ball_ref, *rest, nb, ts, final):
    (y_ref, buf_ref, sout_ref, full, cvt) = rest[-5:]
    r = nb * ts
    x = x_ref[...].reshape(r, D_MODEL)
    hb = _rms_rows(x, ng_ref[...]).astype(BF16)

    a = _dot(hb, winb[:, OFF_A:OFF_A + D_CONV])
    gl = _dot(hb, winb[:, OFF_GL:OFF_GL + D_CONV])
    u = a * _sigmoid(gl)
    for g in range(N_LANE_GROUPS):
        lanes = slice(g * LANE, (g + 1) * LANE)
        for t in range(HIST):
            full[g, t * SLAB:t * SLAB + nb, :] = cache_ref[t, :, lanes]
        for b in range(nb):
            full[g, pl.ds(HIST * SLAB + b, ts, stride=SLAB), :] = u[b * ts:(b + 1) * ts, lanes]
    for g in range(N_LANE_GROUPS):
        w = [cwb_ref[k, :, g * LANE:(g + 1) * LANE] for k in range(CONV_W)]
        for t in range(ts):
            acc = None
            for k in range(CONV_W):
                slab = full[g, (t + k) * SLAB:(t + k) * SLAB + nb, :].reshape(nb // SUBLANE, SUBLANE, LANE)
                acc = slab * w[k] if acc is None else acc + slab * w[k]
            cvt[g, t * SLAB:t * SLAB + nb, :] = acc.reshape(nb, LANE)
    for t in range(HIST):
        buf_ref[t] = jnp.concatenate(
            [full[g, (ts + t) * SLAB:(ts + t) * SLAB + nb, :] for g in range(N_LANE_GROUPS)], axis=-1)
    cv = jnp.concatenate(
        [jnp.concatenate([cvt[g, pl.ds(b, ts, stride=SLAB), :] for b in range(nb)], axis=0)
         for g in range(N_LANE_GROUPS)], axis=-1)
    zc = _dot(hb, winb[:, OFF_ZC:OFF_ZC + D_CONV])
    cvo = _conv_post(cv, zc, cb_ref, cng_ref, cnb_ref, gm_ref, wpwb, bpw_ref)

    q = _dot(hb, winb[:, OFF_Q:OFF_Q + D_K]) * (HEAD_K ** -0.5)
    k = _dot(hb, winb[:, OFF_K:OFF_K + D_K])
    v = _dot(hb, winb[:, OFF_V:OFF_V + D_V]).astype(BF16)
    g = _gate_log(hb, wlrt_ref, wal_ref, bal_ref)
    gcum = _sum_matmul(ltri_ref[...], g)
    gtot = _sum_matmul(ball_ref[...], g)
    xq = q * jnp.exp(gcum)
    yk = (k * jnp.exp(-gcum)).astype(BF16)
    kdt = (k * jnp.exp(gtot - gcum)).T.astype(BF16)
    gt = g.T

    row = lax.broadcasted_iota(jnp.int32, (r, r), 0)
    col = lax.broadcasted_iota(jnp.int32, (r, r), 1)
    causal = jnp.logical_and(col <= row, (col // ts) == (row // ts))
    head_lo = lax.broadcasted_iota(jnp.int32, (r, LANE), 1) < HEAD_K
    seq_sel = (lax.broadcasted_iota(jnp.int32, (nb, 1, r), 2) // ts
               == lax.broadcasted_iota(jnp.int32, (nb, 1, r), 0))
    blk = (lax.broadcasted_iota(jnp.int32, (r, nb * LANE), 1) // LANE
           == lax.broadcasted_iota(jnp.int32, (r, nb * LANE), 0) // ts)

    s0 = s0_ref[...]
    s0_bf = s0.astype(BF16)
    gsum = jnp.sum(jnp.where(seq_sel, gt[None, :, :], 0.0), axis=-1, keepdims=True)
    s_new = jnp.exp(gsum) * s0

    outs = []
    kvs = []
    for h in range(HEADS):
        j, hi = divmod(h, 2)
        pair = slice(j * LANE, (j + 1) * LANE)
        hm = head_lo if hi == 0 else jnp.logical_not(head_lo)
        xm = jnp.where(hm, xq[:, pair], 0.0).astype(BF16)
        att = jnp.where(causal, _dot_nt(xm, yk[:, pair]), 0.0).astype(BF16)
        vh = v[:, h * HEAD_V:(h + 1) * HEAD_V]
        xblk = jnp.where(blk, jnp.concatenate([xm] * nb, axis=-1), jnp.zeros((), BF16))
        s_pair = s0_bf[:, pair, :].reshape(nb * LANE, HEAD_V)
        outs.append(_dot(att, vh) + _dot(xblk, s_pair))
        kh = kdt[h * HEAD_K:(h + 1) * HEAD_K, :]
        kblk = jnp.where(seq_sel, kh[None, :, :], jnp.zeros((), BF16)).reshape(nb * HEAD_K, r)
        kvs.append(_dot(kblk, vh).reshape(nb, HEAD_K, HEAD_V))
    sout_ref[...] = s_new + jnp.concatenate(kvs, axis=1)

    o = jnp.concatenate(outs, axis=-1)
    zg = _dot(hb, winb[:, OFF_ZG:OFF_ZG + D_V])
    cat = jnp.concatenate([cvo.astype(BF16), _gla_post(o, zg, gg_ref).astype(BF16)], axis=-1)
    y_ref[...] = _finish(x, cat, woutb, fg_ref, final).reshape(nb, ts, D_MODEL)


def _sample_layer(l, x, cache, s0, lw, prev, final):
    b, ts, _ = x.shape
    nb = SAMPLE_SEQS
    assert b % nb == 0 and ts == SUBLANE and nb * ts == LANE
    r = nb * ts
    ops, specs = _operands(l, lw, (_block_tri(r, ts), _block_ones(r, ts)), prev)
    state_specs = [pl.BlockSpec((None, HIST, nb, D_CONV), lambda i: (l, 0, i, 0)),
                   pl.BlockSpec((None, nb, D_K, HEAD_V), lambda i: (l, i, 0, 0))]
    in_specs = [pl.BlockSpec((nb, ts, D_MODEL), lambda i: (i, 0, 0))] + state_specs + specs
    out_shape = (jax.ShapeDtypeStruct((b, ts, D_MODEL), F32),
                 jax.ShapeDtypeStruct(cache.shape, F32),
                 jax.ShapeDtypeStruct(s0.shape, F32))
    out_specs = [pl.BlockSpec((nb, ts, D_MODEL), lambda i: (i, 0, 0))] + state_specs
    n_in = 3 + len(ops)
    aliases = {n_in - len(prev) + k: 1 + k for k in range(len(prev))}
    return pl.pallas_call(
        functools.partial(_sample_kernel, nb=nb, ts=ts, final=final),
        grid=(b // nb,),
        in_specs=in_specs, out_specs=out_specs, out_shape=out_shape,
        scratch_shapes=[pltpu.VMEM((N_LANE_GROUPS, (HIST + ts) * SLAB, LANE), F32),
                        pltpu.VMEM((N_LANE_GROUPS, ts * SLAB, LANE), F32)],
        input_output_aliases=aliases,
        compiler_params=pltpu.CompilerParams(
            dimension_semantics=("arbitrary",), vmem_limit_bytes=VMEM_LIMIT),
        name="sample_layer",
    )(x, cache, s0, *ops)


def _lower_tri(n):
    return jnp.asarray(np.tril(np.ones((n, n), np.float32)), BF16)


def _block_tri(n, blk):
    i = np.arange(n)
    m = (i[None, :] <= i[:, None]) & (i[None, :] // blk == i[:, None] // blk)
    return jnp.asarray(m.astype(np.float32), BF16)


def _block_ones(n, blk):
    i = np.arange(n)
    return jnp.asarray((i[None, :] // blk == i[:, None] // blk).astype(np.float32), BF16)


def _group_mean_matrix():
    gsz = D_CONV // CONV_GROUPS
    i = np.arange(D_CONV)
    m = (i[None, :] // gsz == i[:, None] // gsz).astype(np.float32) / gsz
    return jnp.asarray(m, BF16)


def _layer_weights(l, winb, wlrt, wpwb, woutb, norm_g, w_alpha, b_alpha, conv_w, conv_b, cn_g,
                   cn_b, b_pw, gla_g, final_g):
    row = lambda a: a.reshape(1, -1).astype(F32)
    wal = jnp.pad(w_alpha[l], ((0, LR_PAD - GATE_RANK), (0, 0))).astype(BF16)
    cwb = jnp.broadcast_to(conv_w[l][:, None, :], (CONV_W, SUBLANE, D_CONV)).astype(F32)
    return (row(norm_g[l]), winb, wlrt, wal, row(b_alpha[l]), cwb, row(conv_b[l]), row(cn_g[l]),
            row(cn_b[l]), _group_mean_matrix(), wpwb, row(b_pw[l]),
            row(gla_g[l]), woutb, row(final_g))


LAYER_STACKED = (1, 2, 10, 13)


def kernel(x_prompt, x_sample, cache_conv, state_gla, norm_g, w_in, w_alpha, b_alpha, conv_w,
           conv_b, cn_g, cn_b, w_pw, b_pw, gla_g, w_out, final_g):
    depth = w_in.shape[0]
    n_seq, n_dec = x_prompt.shape[0], x_sample.shape[0]
    state_in = state_gla.reshape(depth, n_dec, D_K, HEAD_V)
    cache_in = jnp.swapaxes(cache_conv, 1, 2)
    winb, wlrt = _input_projection_bf16(w_in)
    wpwb, woutb = w_pw.astype(BF16), w_out.astype(BF16)
    hp, hs = x_prompt, x_sample
    prev_p, prev_s = (), ()
    for l in range(depth):
        lw = _layer_weights(l, winb, wlrt, wpwb, woutb, norm_g, w_alpha, b_alpha, conv_w, conv_b,
                            cn_g, cn_b, b_pw, gla_g, final_g)
        final = l == depth - 1
        hp, *prev_p = _prompt_layer(l, depth, hp, lw, prev_p, final)
        hs, *prev_s = _sample_layer(l, hs, cache_in, state_in, lw, prev_s, final)
    conv_p, gla_p = prev_p
    conv_s, gla_s = prev_s
    return (hp, hs, conv_p, gla_p.reshape(depth, n_seq, HEADS, HEAD_K, HEAD_V),
            jnp.swapaxes(conv_s, 1, 2), gla_s.reshape(depth, n_dec, HEADS, HEAD_K, HEAD_V))
```

```python
import functools

import jax
import jax.numpy as jnp
import numpy as np
from jax import lax
from jax.experimental import pallas as pl
from jax.experimental.pallas import tpu as pltpu

F32 = jnp.float32
BF16 = jnp.bfloat16

D_MODEL = 1024
D_CONV = 512
CONV_GROUPS = 8
CONV_W = 31
HIST = CONV_W - 1
D_V = 512
HEADS = 4
D_K = 256
HEAD_K = 64
HEAD_V = 128
GATE_RANK = 16
GATE_NORM = 16.0
EPS = 1e-6

LANE = 128
SUBLANE = 8
HIST_PAD = 32
N_LANE_GROUPS = D_CONV // LANE
LR_PAD = LANE

OFF_A, OFF_GL, OFF_ZC = 0, 512, 1024
OFF_Q, OFF_K, OFF_V, OFF_ZG, OFF_LR = 1536, 1792, 2048, 2560, 3072
PROJ_COLS = OFF_LR - OFF_ZC

GLA_CHUNK = 128
PROMPT_TILE = 512
PROMPT_SEQS = 1
PROMPT_SEG = PROMPT_TILE // SUBLANE
UHALO0 = SUBLANE
UDATA0 = UHALO0 + HIST_PAD
USEG = UDATA0 + PROMPT_SEG
CSEG = PROMPT_SEG + SUBLANE
assert (USEG // SUBLANE) % 2 == 1 and (CSEG // SUBLANE) % 2 == 1
UTAIL = SUBLANE * USEG - HIST_PAD
SAMPLE_SEQS = 16
SLAB = SAMPLE_SEQS + SUBLANE
assert (SLAB // SUBLANE) % 2 == 1
CONV_UNROLL = 8
WPREP_ROWS = OFF_ZC
assert OFF_LR % WPREP_ROWS == 0 and OFF_GL - OFF_A == D_CONV and OFF_ZC - OFF_GL == D_CONV

VMEM_LIMIT = 56 * 1024 * 1024


def _dot(a, b):
    return jnp.dot(a, b, preferred_element_type=F32)


def _dot_nt(a, b):
    return lax.dot_general(a, b, (((1,), (1,)), ((), ())), preferred_element_type=F32)


def _sigmoid(x):
    return 1.0 / (1.0 + jnp.exp(-x))


def _silu(x):
    return x * _sigmoid(x)


def _log_sigmoid(x):
    return jnp.minimum(x, 0.0) - jnp.log(1.0 + jnp.exp(-jnp.abs(x)))


def _rms_rows(x, g):
    ms = jnp.mean(x * x, axis=-1, keepdims=True)
    return x * lax.rsqrt(ms + EPS) * g


def _split3(g):
    hi = g.astype(BF16)
    r1 = g - hi.astype(F32)
    mid = r1.astype(BF16)
    lo = (r1 - mid.astype(F32)).astype(BF16)
    return hi, mid, lo


def _sum_matmul(m, g):
    hi, mid, lo = _split3(g)
    return _dot(m, hi) + _dot(m, mid) + _dot(m, lo)


def _conv_post(cv, zc, cb_ref, cng_ref, cnb_ref, gm_ref, wpw_ref, bpw_ref):
    cv = cv + cb_ref[...]
    gm = gm_ref[...]
    mu = _dot(cv.astype(BF16), gm)
    d = cv - mu
    var = _dot((d * d).astype(BF16), gm)
    yn = d * lax.rsqrt(var + EPS) * cng_ref[...] + cnb_ref[...]
    pw = _dot(_silu(yn).astype(BF16), wpw_ref[...]) + bpw_ref[...]
    return pw * _silu(zc)


def _glu_group(hb, winb, g):
    ag = _dot(hb, winb[:, g * 2 * LANE:(g + 1) * 2 * LANE])
    return ag[:, 0:LANE] * _sigmoid(ag[:, LANE:])


def _gate_log(hb, wlrt_ref, wal_ref, bal_ref):
    lr = _dot_nt(hb, wlrt_ref[...])
    z = _dot(lr.astype(BF16), wal_ref[...]) + bal_ref[...]
    return _log_sigmoid(z) * (1.0 / GATE_NORM)


def _gla_post(o, zg, gg_ref):
    parts = []
    for h in range(HEADS):
        oh = o[:, h * HEAD_V:(h + 1) * HEAD_V]
        ms = jnp.mean(oh * oh, axis=-1, keepdims=True)
        parts.append(oh * lax.rsqrt(ms + EPS) * gg_ref[...])
    return jnp.concatenate(parts, axis=-1) * _silu(zg)


def _finish(x, cat, wout_ref, fg_ref, final):
    y = x + _dot(cat, wout_ref[...])
    if final:
        y = _rms_rows(y, fg_ref[...])
    return y


def _wprep_kernel(wt_ref, lrt_ref, o_ref, olr_ref):
    w = wt_ref[...].T.astype(BF16)

    @pl.when(pl.program_id(1) == 0)
    def _():
        for g in range(N_LANE_GROUPS):
            o_ref[:, 2 * g * LANE:(2 * g + 1) * LANE] = w[:, OFF_A + g * LANE:OFF_A + (g + 1) * LANE]
            o_ref[:, (2 * g + 1) * LANE:(2 * g + 2) * LANE] = w[:, OFF_GL + g * LANE:OFF_GL + (g + 1) * LANE]
        olr_ref[...] = jnp.zeros_like(olr_ref)
        olr_ref[0:GATE_RANK, :] = lrt_ref[...].astype(BF16)

    @pl.when(pl.program_id(1) > 0)
    def _():
        o_ref[...] = w


def _input_projection_bf16(w_in):
    depth = w_in.shape[0]
    wt = jnp.swapaxes(w_in, 1, 2)
    return pl.pallas_call(
        _wprep_kernel,
        grid=(depth, OFF_LR // WPREP_ROWS),
        in_specs=[pl.BlockSpec((None, WPREP_ROWS, D_MODEL), lambda l, c: (l, c, 0)),
                  pl.BlockSpec((None, GATE_RANK, D_MODEL), lambda l, c: (l, OFF_LR // GATE_RANK, 0))],
        out_specs=[pl.BlockSpec((None, D_MODEL, WPREP_ROWS), lambda l, c: (l, 0, c)),
                   pl.BlockSpec((None, LR_PAD, D_MODEL), lambda l, c: (l, 0, 0))],
        out_shape=[jax.ShapeDtypeStruct((depth, D_MODEL, OFF_LR), BF16),
                   jax.ShapeDtypeStruct((depth, LR_PAD, D_MODEL), BF16)],
        compiler_params=pltpu.CompilerParams(dimension_semantics=("arbitrary", "arbitrary")),
        name="weight_layout",
    )(wt, wt)


def _prompt_kernel(x_ref, ng_ref, winb, wlrt_ref, wal_ref, bal_ref, cwb_ref, cb_ref, cng_ref,
                   cnb_ref, gm_ref, wpwb, bpw_ref, gg_ref, woutb, fg_ref, ltri_ref, *rest,
                   ns, tt, final):
    (y_ref, buf_ref, sout_ref, ubuf, cvs, proj, os_, cat, sst) = rest[-9:]
    t = pl.program_id(1)
    seg = tt // SUBLANE
    c = GLA_CHUNK
    half = c // 2

    @pl.when(t == 0)
    def _():
        ubuf[:, :, UTAIL:UTAIL + HIST_PAD, :] = jnp.zeros((ns, N_LANE_GROUPS, HIST_PAD, LANE), F32)
        sst[...] = jnp.zeros_like(sst)

    x = x_ref[...].reshape(ns * tt, D_MODEL)
    hb = _rms_rows(x, ng_ref[...]).astype(BF16)

    for g in range(N_LANE_GROUPS):
        u = _glu_group(hb, winb, g)
        for s in range(ns):
            prev_tail = ubuf[s, g, UTAIL:UTAIL + HIST_PAD, :]
            ext = jnp.concatenate([prev_tail, u[s * tt:(s + 1) * tt, :]], axis=0)
            for j in range(SUBLANE):
                ubuf[s, g, j * USEG + UHALO0:(j + 1) * USEG, :] = ext[j * seg:(j + 1) * seg + HIST_PAD, :]
    gk = _gate_log(hb, wlrt_ref, wal_ref, bal_ref)

    def project(lo, hi):
        proj[:, lo - OFF_ZC:hi - OFF_ZC] = _dot(hb, winb[:, lo:hi])

    project(OFF_Q, OFF_ZG)

    base = UDATA0 - HIST

    def conv_lane_group(g, s):
        w = [cwb_ref[k, :, g * LANE:(g + 1) * LANE] for k in range(CONV_W)]
        for i0 in range(0, seg, CONV_UNROLL):
            accs = [None] * CONV_UNROLL
            for m in range(CONV_UNROLL + CONV_W - 1):
                um = ubuf[s, g, pl.ds(i0 + base + m, SUBLANE, stride=USEG), :]
                for ii in range(CONV_UNROLL):
                    k = m - ii
                    if 0 <= k < CONV_W:
                        accs[ii] = um * w[k] if accs[ii] is None else accs[ii] + um * w[k]
            for ii in range(CONV_UNROLL):
                cvs[s, g, pl.ds(i0 + ii, SUBLANE, stride=CSEG), :] = accs[ii]

    row = lax.broadcasted_iota(jnp.int32, (c, c), 0)
    col = lax.broadcasted_iota(jnp.int32, (c, c), 1)
    causal2 = jnp.concatenate([col <= row] * 2, axis=1)
    ltri = ltri_ref[...]

    def head_diag(top, bot):
        return jnp.concatenate(
            [jnp.concatenate([top, jnp.zeros_like(top)], axis=1),
             jnp.concatenate([jnp.zeros_like(bot), bot], axis=1)], axis=0)

    def attention_block(r0, s_old):
        rows = slice(r0, r0 + c)
        gcum = _sum_matmul(ltri, gk[rows, :])
        q = proj[rows, OFF_Q - OFF_ZC:OFF_K - OFF_ZC] * (HEAD_K ** -0.5)
        k = proj[rows, OFF_K - OFF_ZC:OFF_V - OFF_ZC]
        v = proj[rows, OFF_V - OFF_ZC:OFF_ZG - OFF_ZC].astype(BF16)
        mid = gcum[half - 1:half, :]
        xq = (q * jnp.exp(gcum - mid)).astype(BF16)
        qg = (q * jnp.exp(gcum)).astype(BF16)
        gt = gcum.T
        kt = k.T
        midc = gt[:, half - 1:half]
        glast = gt[:, c - 1:c]
        ykt = (kt * jnp.exp(midc - gt)).astype(BF16)
        kdt = (kt * jnp.exp(glast - gt)).astype(BF16)
        dcol = jnp.exp(glast)
        s_bf = s_old.astype(BF16)
        outs = []
        s_new = []
        for j in range(HEADS // 2):
            pair = slice(j * LANE, (j + 1) * LANE)
            lo = slice(j * LANE, j * LANE + HEAD_K)
            hi = slice(j * LANE + HEAD_K, (j + 1) * LANE)
            v2 = v[:, 2 * j * HEAD_V:(2 * j + 2) * HEAD_V]
            att = _dot(xq[:, pair], head_diag(ykt[lo, :], ykt[hi, :]))
            att = jnp.where(causal2, att, 0.0).astype(BF16)
            outs.append(_dot(att, head_diag(v2[:, 0:HEAD_V], v2[:, HEAD_V:]))
                        + _dot(qg[:, pair], head_diag(s_bf[lo, :], s_bf[hi, :])))
            kv = _dot(kdt[pair, :], v2)
            s_new.append(dcol[lo, :] * s_old[lo, :] + kv[0:HEAD_K, 0:HEAD_V])
            s_new.append(dcol[hi, :] * s_old[hi, :] + kv[HEAD_K:, HEAD_V:])
        os_[rows, :] = jnp.concatenate(outs, axis=-1)
        return jnp.concatenate(s_new, axis=0)

    fill_cols = D_CONV // 2
    fillers = [(lo, lo + fill_cols) for lo in (OFF_ZC, OFF_ZC + fill_cols, OFF_ZG, OFF_ZG + fill_cols)]
    conv_items = [(g, s) for s in range(ns) for g in range(N_LANE_GROUPS)]
    blocks = [(s, i) for s in range(ns) for i in range(tt // c)]
    states = [sst[s] for s in range(ns)]
    for n in range(max(len(conv_items), len(blocks), len(fillers))):
        if n < len(fillers):
            project(*fillers[n])
        if n < len(conv_items):
            conv_lane_group(*conv_items[n])
        if n < len(blocks):
            s, i = blocks[n]
            states[s] = attention_block(s * tt + i * c, states[s])
    for s in range(ns):
        sst[s] = states[s]

    cv = jnp.concatenate(
        [jnp.concatenate(
            [jnp.concatenate([cvs[s, g, j * CSEG:j * CSEG + seg, :] for j in range(SUBLANE)], axis=0)
             for g in range(N_LANE_GROUPS)], axis=-1) for s in range(ns)], axis=0)
    zc = proj[:, 0:OFF_Q - OFF_ZC]
    cvo = _conv_post(cv, zc, cb_ref, cng_ref, cnb_ref, gm_ref, wpwb, bpw_ref)
    cat[:, 0:D_CONV] = cvo.astype(BF16)

    zg = proj[:, OFF_ZG - OFF_ZC:OFF_LR - OFF_ZC]
    cat[:, D_CONV:] = _gla_post(os_[...], zg, gg_ref).astype(BF16)
    y_ref[...] = _finish(x, cat[...], woutb, fg_ref, final).reshape(ns, tt, D_MODEL)

    @pl.when(t == pl.num_programs(1) - 1)
    def _():
        for s in range(ns):
            buf_ref[s] = jnp.concatenate(
                [ubuf[s, g, UTAIL + HIST_PAD - HIST:UTAIL + HIST_PAD, :]
                 for g in range(N_LANE_GROUPS)], axis=-1)
        sout_ref[...] = sst[...]


def _const_spec(a):
    return pl.BlockSpec(a.shape, lambda *_: (0,) * a.ndim, pipeline_mode=pl.Buffered(1))


def _layer_spec(a, l):
    return pl.BlockSpec((None,) + a.shape[1:], lambda *_: (l,) + (0,) * (a.ndim - 1),
                        pipeline_mode=pl.Buffered(1))


def _operands(l, lw, masks, prev):
    ops = list(lw) + list(masks) + list(prev)
    specs = [_layer_spec(a, l) if i in LAYER_STACKED else _const_spec(a) for i, a in enumerate(lw)]
    specs += [_const_spec(m) for m in masks]
    specs += [pl.BlockSpec(memory_space=pl.ANY) for _ in prev]
    return ops, specs


def _prompt_layer(l, depth, x, lw, prev, final):
    b, t, _ = x.shape
    ns, tt = PROMPT_SEQS, PROMPT_TILE
    assert b % ns == 0 and t % tt == 0 and tt % GLA_CHUNK == 0
    assert tt % (SUBLANE * CONV_UNROLL) == 0 and t >= HIST_PAD
    ops, specs = _operands(l, lw, (_lower_tri(GLA_CHUNK),), prev)
    in_specs = [pl.BlockSpec((ns, tt, D_MODEL), lambda i, j: (i, j, 0))] + specs
    out_shape = (jax.ShapeDtypeStruct((b, t, D_MODEL), F32),
                 jax.ShapeDtypeStruct((depth, b, HIST, D_CONV), F32),
                 jax.ShapeDtypeStruct((depth, b, D_K, HEAD_V), F32))
    out_specs = (pl.BlockSpec((ns, tt, D_MODEL), lambda i, j: (i, j, 0)),
                 pl.BlockSpec((None, ns, HIST, D_CONV), lambda i, j: (l, i, 0, 0)),
                 pl.BlockSpec((None, ns, D_K, HEAD_V), lambda i, j: (l, i, 0, 0)))
    n_in = 1 + len(ops)
    aliases = {n_in - len(prev) + k: 1 + k for k in range(len(prev))}
    scratch = [pltpu.VMEM((ns, N_LANE_GROUPS, SUBLANE * USEG, LANE), F32),
               pltpu.VMEM((ns, N_LANE_GROUPS, SUBLANE * CSEG, LANE), F32),
               pltpu.VMEM((ns * tt, PROJ_COLS), F32),
               pltpu.VMEM((ns * tt, D_V), F32),
               pltpu.VMEM((ns * tt, D_MODEL), BF16),
               pltpu.VMEM((ns, D_K, HEAD_V), F32)]
    return pl.pallas_call(
        functools.partial(_prompt_kernel, ns=ns, tt=tt, final=final),
        grid=(b // ns, t // tt),
        in_specs=in_specs, out_specs=out_specs, out_shape=out_shape,
        scratch_shapes=scratch,
        input_output_aliases=aliases,
        compiler_params=pltpu.CompilerParams(
            dimension_semantics=("arbitrary", "arbitrary"), vmem_limit_bytes=VMEM_LIMIT),
        name="prompt_layer",
    )(x, *ops)


def _sample_kernel(x_ref, cache_ref, s0_ref, ng_ref, winb, wlrt_ref, wal_ref, bal_ref, cwb_ref,
                   cb_ref, cng_ref, cnb_ref, gm_ref, wpwb, bpw_ref, gg_ref, woutb, fg_ref,
                   ltri_ref, ball_ref, *rest, nb, ts, final):
    (y_ref, buf_ref, sout_ref, full, cvt) = rest[-5:]
    r = nb * ts
    x = x_ref[...].reshape(r, D_MODEL)
    hb = _rms_rows(x, ng_ref[...]).astype(BF16)

    for g in range(N_LANE_GROUPS):
        lanes = slice(g * LANE, (g + 1) * LANE)
        u = _glu_group(hb, winb, g)
        for t in range(HIST):
            full[g, t * SLAB:t * SLAB + nb, :] = cache_ref[t, :, lanes]
        for b in range(nb):
            full[g, pl.ds(HIST * SLAB + b, ts, stride=SLAB), :] = u[b * ts:(b + 1) * ts, :]
    for g in range(N_LANE_GROUPS):
        w = [cwb_ref[k, :, g * LANE:(g + 1) * LANE] for k in range(CONV_W)]
        for t in range(ts):
            acc = None
            for k in range(CONV_W):
                slab = full[g, (t + k) * SLAB:(t + k) * SLAB + nb, :].reshape(nb // SUBLANE, SUBLANE, LANE)
                acc = slab * w[k] if acc is None else acc + slab * w[k]
            cvt[g, t * SLAB:t * SLAB + nb, :] = acc.reshape(nb, LANE)
    for t in range(HIST):
        buf_ref[t] = jnp.concatenate(
            [full[g, (ts + t) * SLAB:(ts + t) * SLAB + nb, :] for g in range(N_LANE_GROUPS)], axis=-1)
    cv = jnp.concatenate(
        [jnp.concatenate([cvt[g, pl.ds(b, ts, stride=SLAB), :] for b in range(nb)], axis=0)
         for g in range(N_LANE_GROUPS)], axis=-1)
    zc = _dot(hb, winb[:, OFF_ZC:OFF_ZC + D_CONV])
    cvo = _conv_post(cv, zc, cb_ref, cng_ref, cnb_ref, gm_ref, wpwb, bpw_ref)

    q = _dot(hb, winb[:, OFF_Q:OFF_Q + D_K]) * (HEAD_K ** -0.5)
    k = _dot(hb, winb[:, OFF_K:OFF_K + D_K])
    v = _dot(hb, winb[:, OFF_V:OFF_V + D_V]).astype(BF16)
    g = _gate_log(hb, wlrt_ref, wal_ref, bal_ref)
    gcum = _sum_matmul(ltri_ref[...], g)
    gtot = _sum_matmul(ball_ref[...], g)
    xq = q * jnp.exp(gcum)
    yk = (k * jnp.exp(-gcum)).astype(BF16)
    kdt = (k * jnp.exp(gtot - gcum)).T.astype(BF16)
    gt = g.T

    row = lax.broadcasted_iota(jnp.int32, (r, r), 0)
    col = lax.broadcasted_iota(jnp.int32, (r, r), 1)
    causal = jnp.logical_and(col <= row, (col // ts) == (row // ts))
    head_lo = lax.broadcasted_iota(jnp.int32, (r, LANE), 1) < HEAD_K
    seq_sel = (lax.broadcasted_iota(jnp.int32, (nb, 1, r), 2) // ts
               == lax.broadcasted_iota(jnp.int32, (nb, 1, r), 0))
    blk = (lax.broadcasted_iota(jnp.int32, (r, nb * LANE), 1) // LANE
           == lax.broadcasted_iota(jnp.int32, (r, nb * LANE), 0) // ts)

    s0 = s0_ref[...]
    s0_bf = s0.astype(BF16)
    gsum = jnp.sum(jnp.where(seq_sel, gt[None, :, :], 0.0), axis=-1, keepdims=True)
    s_new = jnp.exp(gsum) * s0

    outs = []
    kvs = []
    for h in range(HEADS):
        j, hi = divmod(h, 2)
        pair = slice(j * LANE, (j + 1) * LANE)
        hm = head_lo if hi == 0 else jnp.logical_not(head_lo)
        xm = jnp.where(hm, xq[:, pair], 0.0).astype(BF16)
        att = jnp.where(causal, _dot_nt(xm, yk[:, pair]), 0.0).astype(BF16)
        vh = v[:, h * HEAD_V:(h + 1) * HEAD_V]
        xblk = jnp.where(blk, jnp.concatenate([xm] * nb, axis=-1), jnp.zeros((), BF16))
        s_pair = s0_bf[:, pair, :].reshape(nb * LANE, HEAD_V)
        outs.append(_dot(att, vh) + _dot(xblk, s_pair))
        kh = kdt[h * HEAD_K:(h + 1) * HEAD_K, :]
        kblk = jnp.where(seq_sel, kh[None, :, :], jnp.zeros((), BF16)).reshape(nb * HEAD_K, r)
        kvs.append(_dot(kblk, vh).reshape(nb, HEAD_K, HEAD_V))
    sout_ref[...] = s_new + jnp.concatenate(kvs, axis=1)

    o = jnp.concatenate(outs, axis=-1)
    zg = _dot(hb, winb[:, OFF_ZG:OFF_ZG + D_V])
    cat = jnp.concatenate([cvo.astype(BF16), _gla_post(o, zg, gg_ref).astype(BF16)], axis=-1)
    y_ref[...] = _finish(x, cat, woutb, fg_ref, final).reshape(nb, ts, D_MODEL)


def _sample_layer(l, x, cache, s0, lw, prev, final):
    b, ts, _ = x.shape
    nb = SAMPLE_SEQS
    assert b % nb == 0 and ts == SUBLANE and nb * ts == LANE
    r = nb * ts
    ops, specs = _operands(l, lw, (_block_tri(r, ts), _block_ones(r, ts)), prev)
    state_specs = [pl.BlockSpec((None, HIST, nb, D_CONV), lambda i: (l, 0, i, 0)),
                   pl.BlockSpec((None, nb, D_K, HEAD_V), lambda i: (l, i, 0, 0))]
    in_specs = [pl.BlockSpec((nb, ts, D_MODEL), lambda i: (i, 0, 0))] + state_specs + specs
    out_shape = (jax.ShapeDtypeStruct((b, ts, D_MODEL), F32),
                 jax.ShapeDtypeStruct(cache.shape, F32),
                 jax.ShapeDtypeStruct(s0.shape, F32))
    out_specs = [pl.BlockSpec((nb, ts, D_MODEL), lambda i: (i, 0, 0))] + state_specs
    n_in = 3 + len(ops)
    aliases = {n_in - len(prev) + k: 1 + k for k in range(len(prev))}
    return pl.pallas_call(
        functools.partial(_sample_kernel, nb=nb, ts=ts, final=final),
        grid=(b // nb,),
        in_specs=in_specs, out_specs=out_specs, out_shape=out_shape,
        scratch_shapes=[pltpu.VMEM((N_LANE_GROUPS, (HIST + ts) * SLAB, LANE), F32),
                        pltpu.VMEM((N_LANE_GROUPS, ts * SLAB, LANE), F32)],
        input_output_aliases=aliases,
        compiler_params=pltpu.CompilerParams(
            dimension_semantics=("arbitrary",), vmem_limit_bytes=VMEM_LIMIT),
        name="sample_layer",
    )(x, cache, s0, *ops)


def _lower_tri(n):
    return jnp.asarray(np.tril(np.ones((n, n), np.float32)), BF16)


def _block_tri(n, blk):
    i = np.arange(n)
    m = (i[None, :] <= i[:, None]) & (i[None, :] // blk == i[:, None] // blk)
    return jnp.asarray(m.astype(np.float32), BF16)


def _block_ones(n, blk):
    i = np.arange(n)
    return jnp.asarray((i[None, :] // blk == i[:, None] // blk).astype(np.float32), BF16)


def _group_mean_matrix():
    gsz = D_CONV // CONV_GROUPS
    i = np.arange(D_CONV)
    m = (i[None, :] // gsz == i[:, None] // gsz).astype(np.float32) / gsz
    return jnp.asarray(m, BF16)


def _layer_weights(l, winb, wlrt, wpwb, woutb, norm_g, w_alpha, b_alpha, conv_w, conv_b, cn_g,
                   cn_b, b_pw, gla_g, final_g):
    row = lambda a: a.reshape(1, -1).astype(F32)
    wal = jnp.pad(w_alpha[l], ((0, LR_PAD - GATE_RANK), (0, 0))).astype(BF16)
    cwb = jnp.broadcast_to(conv_w[l][:, None, :], (CONV_W, SUBLANE, D_CONV)).astype(F32)
    return (row(norm_g[l]), winb, wlrt, wal, row(b_alpha[l]), cwb, row(conv_b[l]), row(cn_g[l]),
            row(cn_b[l]), _group_mean_matrix(), wpwb, row(b_pw[l]),
            row(gla_g[l]), woutb, row(final_g))


LAYER_STACKED = (1, 2, 10, 13)


def kernel(x_prompt, x_sample, cache_conv, state_gla, norm_g, w_in, w_alpha, b_alpha, conv_w,
           conv_b, cn_g, cn_b, w_pw, b_pw, gla_g, w_out, final_g):
    depth = w_in.shape[0]
    n_seq, n_dec = x_prompt.shape[0], x_sample.shape[0]
    state_in = state_gla.reshape(depth, n_dec, D_K, HEAD_V)
    cache_in = jnp.swapaxes(cache_conv, 1, 2)
    winb, wlrt = _input_projection_bf16(w_in)
    wpwb, woutb = w_pw.astype(BF16), w_out.astype(BF16)
    hp, hs = x_prompt, x_sample
    prev_p, prev_s = (), ()
    for l in range(depth):
        lw = _layer_weights(l, winb, wlrt, wpwb, woutb, norm_g, w_alpha, b_alpha, conv_w, conv_b,
                            cn_g, cn_b, b_pw, gla_g, final_g)
        final = l == depth - 1
        hp, *prev_p = _prompt_layer(l, depth, hp, lw, prev_p, final)
        hs, *prev_s = _sample_layer(l, hs, cache_in, state_in, lw, prev_s, final)
    conv_p, gla_p = prev_p
    conv_s, gla_s = prev_s
    return (hp, hs, conv_p, gla_p.reshape(depth, n_seq, HEADS, HEAD_K, HEAD_V),
            jnp.swapaxes(conv_s, 1, 2), gla_s.reshape(depth, n_dec, HEADS, HEAD_K, HEAD_V))
```

```python
import functools

import jax
import jax.numpy as jnp
import numpy as np
from jax import lax
from jax.experimental import pallas as pl
from jax.experimental.pallas import tpu as pltpu

F32 = jnp.float32
BF16 = jnp.bfloat16

D_MODEL = 1024
D_CONV = 512
CONV_GROUPS = 8
CONV_W = 31
HIST = CONV_W - 1
D_V = 512
HEADS = 4
D_K = 256
HEAD_K = 64
HEAD_V = 128
GATE_RANK = 16
GATE_NORM = 16.0
EPS = 1e-6

LANE = 128
SUBLANE = 8
HIST_PAD = 32
N_LANE_GROUPS = D_CONV // LANE
LR_PAD = LANE

OFF_A, OFF_GL, OFF_ZC = 0, 512, 1024
OFF_Q, OFF_K, OFF_V, OFF_ZG, OFF_LR = 1536, 1792, 2048, 2560, 3072
PROJ_COLS = OFF_LR - OFF_ZC

GLA_CHUNK = 128
PROMPT_TILE = 512
PROMPT_SEQS = 1
PROMPT_SEG = PROMPT_TILE // SUBLANE
UHALO0 = SUBLANE
UDATA0 = UHALO0 + HIST_PAD
USEG = UDATA0 + PROMPT_SEG
CSEG = PROMPT_SEG + SUBLANE
assert (USEG // SUBLANE) % 2 == 1 and (CSEG // SUBLANE) % 2 == 1
UTAIL = SUBLANE * USEG - HIST_PAD
SAMPLE_SEQS = 16
SLAB = SAMPLE_SEQS + SUBLANE
assert (SLAB // SUBLANE) % 2 == 1
CONV_UNROLL = 8
GM_COLS = 256
WPREP_ROWS = 1024
assert OFF_LR % WPREP_ROWS == 0

VMEM_LIMIT = 56 * 1024 * 1024


def _dot(a, b):
    return jnp.dot(a, b, preferred_element_type=F32)


def _dot_nt(a, b):
    return lax.dot_general(a, b, (((1,), (1,)), ((), ())), preferred_element_type=F32)


def _sigmoid(x):
    return 1.0 / (1.0 + jnp.exp(-x))


def _silu(x):
    return x * _sigmoid(x)


def _log_sigmoid(x):
    return jnp.minimum(x, 0.0) - jnp.log(1.0 + jnp.exp(-jnp.abs(x)))


def _rms_rows(x, g):
    ms = jnp.mean(x * x, axis=-1, keepdims=True)
    return x * lax.rsqrt(ms + EPS) * g


def _split3(g):
    hi = g.astype(BF16)
    r1 = g - hi.astype(F32)
    mid = r1.astype(BF16)
    lo = (r1 - mid.astype(F32)).astype(BF16)
    return hi, mid, lo


def _sum_matmul(m, g):
    hi, mid, lo = _split3(g)
    return _dot(m, hi) + _dot(m, mid) + _dot(m, lo)


def _conv_post(cv, zc, cb_ref, cng_ref, cnb_ref, gm_ref, wpw_ref, bpw_ref):
    cv = cv + cb_ref[...]
    gm = gm_ref[...]

    def group_mean(v):
        vb = v.astype(BF16)
        return jnp.concatenate([_dot(vb[:, lo:lo + GM_COLS], gm)
                                for lo in range(0, D_CONV, GM_COLS)], axis=-1)

    mu = group_mean(cv)
    d = cv - mu
    var = group_mean(d * d)
    yn = d * lax.rsqrt(var + EPS) * cng_ref[...] + cnb_ref[...]
    pw = _dot(_silu(yn).astype(BF16), wpw_ref[...]) + bpw_ref[...]
    return pw * _silu(zc)


def _gate_log(hb, wlrt_ref, wal_ref, bal_ref):
    lr = _dot_nt(hb, wlrt_ref[...])
    z = _dot(lr.astype(BF16), wal_ref[...]) + bal_ref[...]
    return _log_sigmoid(z) * (1.0 / GATE_NORM)


def _gla_post(o, zg, gg_ref):
    parts = []
    for h in range(HEADS):
        oh = o[:, h * HEAD_V:(h + 1) * HEAD_V]
        ms = jnp.mean(oh * oh, axis=-1, keepdims=True)
        parts.append(oh * lax.rsqrt(ms + EPS) * gg_ref[...])
    return jnp.concatenate(parts, axis=-1) * _silu(zg)


def _finish(x, cat, wout_ref, fg_ref, final):
    y = x + _dot(cat, wout_ref[...])
    if final:
        y = _rms_rows(y, fg_ref[...])
    return y


def _wprep_kernel(wt_ref, lrt_ref, o_ref, olr_ref):
    o_ref[...] = wt_ref[...].T.astype(BF16)

    @pl.when(pl.program_id(1) == 0)
    def _():
        olr_ref[...] = jnp.zeros_like(olr_ref)
        olr_ref[0:GATE_RANK, :] = lrt_ref[...].astype(BF16)


def _input_projection_bf16(w_in):
    depth = w_in.shape[0]
    wt = jnp.swapaxes(w_in, 1, 2)
    return pl.pallas_call(
        _wprep_kernel,
        grid=(depth, OFF_LR // WPREP_ROWS),
        in_specs=[pl.BlockSpec((None, WPREP_ROWS, D_MODEL), lambda l, c: (l, c, 0)),
                  pl.BlockSpec((None, GATE_RANK, D_MODEL), lambda l, c: (l, OFF_LR // GATE_RANK, 0))],
        out_specs=[pl.BlockSpec((None, D_MODEL, WPREP_ROWS), lambda l, c: (l, 0, c)),
                   pl.BlockSpec((None, LR_PAD, D_MODEL), lambda l, c: (l, 0, 0))],
        out_shape=[jax.ShapeDtypeStruct((depth, D_MODEL, OFF_LR), BF16),
                   jax.ShapeDtypeStruct((depth, LR_PAD, D_MODEL), BF16)],
        compiler_params=pltpu.CompilerParams(dimension_semantics=("arbitrary", "arbitrary")),
        name="weight_layout",
    )(wt, wt)


def _prompt_kernel(x_ref, ng_ref, winb, wlrt_ref, wal_ref, bal_ref, cwb_ref, cb_ref, cng_ref,
                   cnb_ref, gm_ref, wpwb, bpw_ref, gg_ref, woutb, fg_ref, ltri_ref, *rest,
                   ns, tt, final):
    (y_ref, buf_ref, sout_ref, ubuf, cvs, proj, os_, cat, sst) = rest[-9:]
    t = pl.program_id(1)
    seg = tt // SUBLANE
    c = GLA_CHUNK
    half = c // 2

    @pl.when(t == 0)
    def _():
        ubuf[:, :, UTAIL:UTAIL + HIST_PAD, :] = jnp.zeros((ns, N_LANE_GROUPS, HIST_PAD, LANE), F32)
        sst[...] = jnp.zeros_like(sst)

    x = x_ref[...].reshape(ns * tt, D_MODEL)
    hb = _rms_rows(x, ng_ref[...]).astype(BF16)

    a = _dot(hb, winb[:, OFF_A:OFF_A + D_CONV])
    gl = _dot(hb, winb[:, OFF_GL:OFF_GL + D_CONV])
    u = a * _sigmoid(gl)
    for s in range(ns):
        for g in range(N_LANE_GROUPS):
            prev_tail = ubuf[s, g, UTAIL:UTAIL + HIST_PAD, :]
            ext = jnp.concatenate([prev_tail, u[s * tt:(s + 1) * tt, g * LANE:(g + 1) * LANE]], axis=0)
            for j in range(SUBLANE):
                ubuf[s, g, j * USEG + UHALO0:(j + 1) * USEG, :] = ext[j * seg:(j + 1) * seg + HIST_PAD, :]
    gk = _gate_log(hb, wlrt_ref, wal_ref, bal_ref)

    def project(lo, hi):
        proj[:, lo - OFF_ZC:hi - OFF_ZC] = _dot(hb, winb[:, lo:hi])

    project(OFF_Q, OFF_ZG)

    base = UDATA0 - HIST

    def conv_lane_group(g, s):
        w = [cwb_ref[k, :, g * LANE:(g + 1) * LANE] for k in range(CONV_W)]
        for i0 in range(0, seg, CONV_UNROLL):
            accs = [None] * CONV_UNROLL
            for m in range(CONV_UNROLL + CONV_W - 1):
                um = ubuf[s, g, pl.ds(i0 + base + m, SUBLANE, stride=USEG), :]
                for ii in range(CONV_UNROLL):
                    k = m - ii
                    if 0 <= k < CONV_W:
                        accs[ii] = um * w[k] if accs[ii] is None else accs[ii] + um * w[k]
            for ii in range(CONV_UNROLL):
                cvs[s, g, pl.ds(i0 + ii, SUBLANE, stride=CSEG), :] = accs[ii]

    row = lax.broadcasted_iota(jnp.int32, (c, c), 0)
    col = lax.broadcasted_iota(jnp.int32, (c, c), 1)
    causal2 = jnp.concatenate([col <= row] * 2, axis=1)
    ltri = ltri_ref[...]

    def head_diag(top, bot):
        return jnp.concatenate(
            [jnp.concatenate([top, jnp.zeros_like(top)], axis=1),
             jnp.concatenate([jnp.zeros_like(bot), bot], axis=1)], axis=0)

    def attention_block(r0, s_old):
        rows = slice(r0, r0 + c)
        gcum = _sum_matmul(ltri, gk[rows, :])
        q = proj[rows, OFF_Q - OFF_ZC:OFF_K - OFF_ZC] * (HEAD_K ** -0.5)
        k = proj[rows, OFF_K - OFF_ZC:OFF_V - OFF_ZC]
        v = proj[rows, OFF_V - OFF_ZC:OFF_ZG - OFF_ZC].astype(BF16)
        mid = gcum[half - 1:half, :]
        xq = (q * jnp.exp(gcum - mid)).astype(BF16)
        qg = (q * jnp.exp(gcum)).astype(BF16)
        gt = gcum.T
        kt = k.T
        midc = gt[:, half - 1:half]
        glast = gt[:, c - 1:c]
        ykt = (kt * jnp.exp(midc - gt)).astype(BF16)
        kdt = (kt * jnp.exp(glast - gt)).astype(BF16)
        dcol = jnp.exp(glast)
        s_bf = s_old.astype(BF16)
        outs = []
        s_new = []
        for j in range(HEADS // 2):
            pair = slice(j * LANE, (j + 1) * LANE)
            lo = slice(j * LANE, j * LANE + HEAD_K)
            hi = slice(j * LANE + HEAD_K, (j + 1) * LANE)
            v2 = v[:, 2 * j * HEAD_V:(2 * j + 2) * HEAD_V]
            att = _dot(xq[:, pair], head_diag(ykt[lo, :], ykt[hi, :]))
            att = jnp.where(causal2, att, 0.0).astype(BF16)
            outs.append(_dot(att, head_diag(v2[:, 0:HEAD_V], v2[:, HEAD_V:]))
                        + _dot(qg[:, pair], head_diag(s_bf[lo, :], s_bf[hi, :])))
            kv = _dot(kdt[pair, :], v2)
            s_new.append(dcol[lo, :] * s_old[lo, :] + kv[0:HEAD_K, 0:HEAD_V])
            s_new.append(dcol[hi, :] * s_old[hi, :] + kv[HEAD_K:, HEAD_V:])
        os_[rows, :] = jnp.concatenate(outs, axis=-1)
        return jnp.concatenate(s_new, axis=0)

    fill_cols = D_CONV // 2
    fillers = [(lo, lo + fill_cols) for lo in (OFF_ZC, OFF_ZC + fill_cols, OFF_ZG, OFF_ZG + fill_cols)]
    conv_items = [(g, s) for s in range(ns) for g in range(N_LANE_GROUPS)]
    blocks = [(s, i) for s in range(ns) for i in range(tt // c)]
    states = [sst[s] for s in range(ns)]
    for n in range(max(len(conv_items), len(blocks), len(fillers))):
        if n < len(fillers):
            project(*fillers[n])
        if n < len(conv_items):
            conv_lane_group(*conv_items[n])
        if n < len(blocks):
            s, i = blocks[n]
            states[s] = attention_block(s * tt + i * c, states[s])
    for s in range(ns):
        sst[s] = states[s]

    cv = jnp.concatenate(
        [jnp.concatenate(
            [jnp.concatenate([cvs[s, g, j * CSEG:j * CSEG + seg, :] for j in range(SUBLANE)], axis=0)
             for g in range(N_LANE_GROUPS)], axis=-1) for s in range(ns)], axis=0)
    zc = proj[:, 0:OFF_Q - OFF_ZC]
    cvo = _conv_post(cv, zc, cb_ref, cng_ref, cnb_ref, gm_ref, wpwb, bpw_ref)
    cat[:, 0:D_CONV] = cvo.astype(BF16)

    zg = proj[:, OFF_ZG - OFF_ZC:OFF_LR - OFF_ZC]
    cat[:, D_CONV:] = _gla_post(os_[...], zg, gg_ref).astype(BF16)
    y_ref[...] = _finish(x, cat[...], woutb, fg_ref, final).reshape(ns, tt, D_MODEL)

    @pl.when(t == pl.num_programs(1) - 1)
    def _():
        for s in range(ns):
            buf_ref[s] = jnp.concatenate(
                [ubuf[s, g, UTAIL + HIST_PAD - HIST:UTAIL + HIST_PAD, :]
                 for g in range(N_LANE_GROUPS)], axis=-1)
        sout_ref[...] = sst[...]


def _const_spec(a):
    return pl.BlockSpec(a.shape, lambda *_: (0,) * a.ndim, pipeline_mode=pl.Buffered(1))


def _layer_spec(a, l):
    return pl.BlockSpec((None,) + a.shape[1:], lambda *_: (l,) + (0,) * (a.ndim - 1),
                        pipeline_mode=pl.Buffered(1))


def _operands(l, lw, masks, prev):
    ops = list(lw) + list(masks) + list(prev)
    specs = [_layer_spec(a, l) if i in LAYER_STACKED else _const_spec(a) for i, a in enumerate(lw)]
    specs += [_const_spec(m) for m in masks]
    specs += [pl.BlockSpec(memory_space=pl.ANY) for _ in prev]
    return ops, specs


def _prompt_layer(l, depth, x, lw, prev, final):
    b, t, _ = x.shape
    ns, tt = PROMPT_SEQS, PROMPT_TILE
    assert b % ns == 0 and t % tt == 0 and tt % GLA_CHUNK == 0
    assert tt % (SUBLANE * CONV_UNROLL) == 0 and t >= HIST_PAD
    ops, specs = _operands(l, lw, (_lower_tri(GLA_CHUNK),), prev)
    in_specs = [pl.BlockSpec((ns, tt, D_MODEL), lambda i, j: (i, j, 0))] + specs
    out_shape = (jax.ShapeDtypeStruct((b, t, D_MODEL), F32),
                 jax.ShapeDtypeStruct((depth, b, HIST, D_CONV), F32),
                 jax.ShapeDtypeStruct((depth, b, D_K, HEAD_V), F32))
    out_specs = (pl.BlockSpec((ns, tt, D_MODEL), lambda i, j: (i, j, 0)),
                 pl.BlockSpec((None, ns, HIST, D_CONV), lambda i, j: (l, i, 0, 0)),
                 pl.BlockSpec((None, ns, D_K, HEAD_V), lambda i, j: (l, i, 0, 0)))
    n_in = 1 + len(ops)
    aliases = {n_in - len(prev) + k: 1 + k for k in range(len(prev))}
    scratch = [pltpu.VMEM((ns, N_LANE_GROUPS, SUBLANE * USEG, LANE), F32),
               pltpu.VMEM((ns, N_LANE_GROUPS, SUBLANE * CSEG, LANE), F32),
               pltpu.VMEM((ns * tt, PROJ_COLS), F32),
               pltpu.VMEM((ns * tt, D_V), F32),
               pltpu.VMEM((ns * tt, D_MODEL), BF16),
               pltpu.VMEM((ns, D_K, HEAD_V), F32)]
    return pl.pallas_call(
        functools.partial(_prompt_kernel, ns=ns, tt=tt, final=final),
        grid=(b // ns, t // tt),
        in_specs=in_specs, out_specs=out_specs, out_shape=out_shape,
        scratch_shapes=scratch,
        input_output_aliases=aliases,
        compiler_params=pltpu.CompilerParams(
            dimension_semantics=("arbitrary", "arbitrary"), vmem_limit_bytes=VMEM_LIMIT),
        name="prompt_layer",
    )(x, *ops)


def _sample_kernel(x_ref, cache_ref, s0_ref, ng_ref, winb, wlrt_ref, wal_ref, bal_ref, cwb_ref,
                   cb_ref, cng_ref, cnb_ref, gm_ref, wpwb, bpw_ref, gg_ref, woutb, fg_ref,
                   ltri_ref, ball_ref, *rest, nb, ts, final):
    (y_ref, buf_ref, sout_ref, full, cvt) = rest[-5:]
    r = nb * ts
    x = x_ref[...].reshape(r, D_MODEL)
    hb = _rms_rows(x, ng_ref[...]).astype(BF16)

    a = _dot(hb, winb[:, OFF_A:OFF_A + D_CONV])
    gl = _dot(hb, winb[:, OFF_GL:OFF_GL + D_CONV])
    u = a * _sigmoid(gl)
    for g in range(N_LANE_GROUPS):
        lanes = slice(g * LANE, (g + 1) * LANE)
        for t in range(HIST):
            full[g, t * SLAB:t * SLAB + nb, :] = cache_ref[t, :, lanes]
        for b in range(nb):
            full[g, pl.ds(HIST * SLAB + b, ts, stride=SLAB), :] = u[b * ts:(b + 1) * ts, lanes]
    for g in range(N_LANE_GROUPS):
        w = [cwb_ref[k, :, g * LANE:(g + 1) * LANE] for k in range(CONV_W)]
        for t in range(ts):
            acc = None
            for k in range(CONV_W):
                slab = full[g, (t + k) * SLAB:(t + k) * SLAB + nb, :].reshape(nb // SUBLANE, SUBLANE, LANE)
                acc = slab * w[k] if acc is None else acc + slab * w[k]
            cvt[g, t * SLAB:t * SLAB + nb, :] = acc.reshape(nb, LANE)
    for t in range(HIST):
        buf_ref[t] = jnp.concatenate(
            [full[g, (ts + t) * SLAB:(ts + t) * SLAB + nb, :] for g in range(N_LANE_GROUPS)], axis=-1)
    cv = jnp.concatenate(
        [jnp.concatenate([cvt[g, pl.ds(b, ts, stride=SLAB), :] for b in range(nb)], axis=0)
         for g in range(N_LANE_GROUPS)], axis=-1)
    zc = _dot(hb, winb[:, OFF_ZC:OFF_ZC + D_CONV])
    cvo = _conv_post(cv, zc, cb_ref, cng_ref, cnb_ref, gm_ref, wpwb, bpw_ref)

    q = _dot(hb, winb[:, OFF_Q:OFF_Q + D_K]) * (HEAD_K ** -0.5)
    k = _dot(hb, winb[:, OFF_K:OFF_K + D_K])
    v = _dot(hb, winb[:, OFF_V:OFF_V + D_V]).astype(BF16)
    g = _gate_log(hb, wlrt_ref, wal_ref, bal_ref)
    gcum = _sum_matmul(ltri_ref[...], g)
    gtot = _sum_matmul(ball_ref[...], g)
    xq = q * jnp.exp(gcum)
    yk = (k * jnp.exp(-gcum)).astype(BF16)
    kdt = (k * jnp.exp(gtot - gcum)).T.astype(BF16)
    gt = g.T

    row = lax.broadcasted_iota(jnp.int32, (r, r), 0)
    col = lax.broadcasted_iota(jnp.int32, (r, r), 1)
    causal = jnp.logical_and(col <= row, (col // ts) == (row // ts))
    head_lo = lax.broadcasted_iota(jnp.int32, (r, LANE), 1) < HEAD_K
    seq_sel = (lax.broadcasted_iota(jnp.int32, (nb, 1, r), 2) // ts
               == lax.broadcasted_iota(jnp.int32, (nb, 1, r), 0))
    blk = (lax.broadcasted_iota(jnp.int32, (r, nb * LANE), 1) // LANE
           == lax.broadcasted_iota(jnp.int32, (r, nb * LANE), 0) // ts)

    s0 = s0_ref[...]
    s0_bf = s0.astype(BF16)
    gsum = jnp.sum(jnp.where(seq_sel, gt[None, :, :], 0.0), axis=-1, keepdims=True)
    s_new = jnp.exp(gsum) * s0

    outs = []
    kvs = []
    for h in range(HEADS):
        j, hi = divmod(h, 2)
        pair = slice(j * LANE, (j + 1) * LANE)
        hm = head_lo if hi == 0 else jnp.logical_not(head_lo)
        xm = jnp.where(hm, xq[:, pair], 0.0).astype(BF16)
        att = jnp.where(causal, _dot_nt(xm, yk[:, pair]), 0.0).astype(BF16)
        vh = v[:, h * HEAD_V:(h + 1) * HEAD_V]
        xblk = jnp.where(blk, jnp.concatenate([xm] * nb, axis=-1), jnp.zeros((), BF16))
        s_pair = s0_bf[:, pair, :].reshape(nb * LANE, HEAD_V)
        outs.append(_dot(att, vh) + _dot(xblk, s_pair))
        kh = kdt[h * HEAD_K:(h + 1) * HEAD_K, :]
        kblk = jnp.where(seq_sel, kh[None, :, :], jnp.zeros((), BF16)).reshape(nb * HEAD_K, r)
        kvs.append(_dot(kblk, vh).reshape(nb, HEAD_K, HEAD_V))
    sout_ref[...] = s_new + jnp.concatenate(kvs, axis=1)

    o = jnp.concatenate(outs, axis=-1)
    zg = _dot(hb, winb[:, OFF_ZG:OFF_ZG + D_V])
    cat = jnp.concatenate([cvo.astype(BF16), _gla_post(o, zg, gg_ref).astype(BF16)], axis=-1)
    y_ref[...] = _finish(x, cat, woutb, fg_ref, final).reshape(nb, ts, D_MODEL)


def _sample_layer(l, x, cache, s0, lw, prev, final):
    b, ts, _ = x.shape
    nb = SAMPLE_SEQS
    assert b % nb == 0 and ts == SUBLANE and nb * ts == LANE
    r = nb * ts
    ops, specs = _operands(l, lw, (_block_tri(r, ts), _block_ones(r, ts)), prev)
    state_specs = [pl.BlockSpec((None, HIST, nb, D_CONV), lambda i: (l, 0, i, 0)),
                   pl.BlockSpec((None, nb, D_K, HEAD_V), lambda i: (l, i, 0, 0))]
    in_specs = [pl.BlockSpec((nb, ts, D_MODEL), lambda i: (i, 0, 0))] + state_specs + specs
    out_shape = (jax.ShapeDtypeStruct((b, ts, D_MODEL), F32),
                 jax.ShapeDtypeStruct(cache.shape, F32),
                 jax.ShapeDtypeStruct(s0.shape, F32))
    out_specs = [pl.BlockSpec((nb, ts, D_MODEL), lambda i: (i, 0, 0))] + state_specs
    n_in = 3 + len(ops)
    aliases = {n_in - len(prev) + k: 1 + k for k in range(len(prev))}
    return pl.pallas_call(
        functools.partial(_sample_kernel, nb=nb, ts=ts, final=final),
        grid=(b // nb,),
        in_specs=in_specs, out_specs=out_specs, out_shape=out_shape,
        scratch_shapes=[pltpu.VMEM((N_LANE_GROUPS, (HIST + ts) * SLAB, LANE), F32),
                        pltpu.VMEM((N_LANE_GROUPS, ts * SLAB, LANE), F32)],
        input_output_aliases=aliases,
        compiler_params=pltpu.CompilerParams(
            dimension_semantics=("arbitrary",), vmem_limit_bytes=VMEM_LIMIT),
        name="sample_layer",
    )(x, cache, s0, *ops)


def _lower_tri(n):
    return jnp.asarray(np.tril(np.ones((n, n), np.float32)), BF16)


def _block_tri(n, blk):
    i = np.arange(n)
    m = (i[None, :] <= i[:, None]) & (i[None, :] // blk == i[:, None] // blk)
    return jnp.asarray(m.astype(np.float32), BF16)


def _block_ones(n, blk):
    i = np.arange(n)
    return jnp.asarray((i[None, :] // blk == i[:, None] // blk).astype(np.float32), BF16)


def _group_mean_matrix():
    gsz = D_CONV // CONV_GROUPS
    assert GM_COLS % gsz == 0 and D_CONV % GM_COLS == 0
    i = np.arange(GM_COLS)
    m = (i[None, :] // gsz == i[:, None] // gsz).astype(np.float32) / gsz
    return jnp.asarray(m, BF16)


def _layer_weights(l, winb, wlrt, wpwb, woutb, norm_g, w_alpha, b_alpha, conv_w, conv_b, cn_g,
                   cn_b, b_pw, gla_g, final_g):
    row = lambda a: a.reshape(1, -1).astype(F32)
    wal = jnp.pad(w_alpha[l], ((0, LR_PAD - GATE_RANK), (0, 0))).astype(BF16)
    cwb = jnp.broadcast_to(conv_w[l][:, None, :], (CONV_W, SUBLANE, D_CONV)).astype(F32)
    return (row(norm_g[l]), winb, wlrt, wal, row(b_alpha[l]), cwb, row(conv_b[l]), row(cn_g[l]),
            row(cn_b[l]), _group_mean_matrix(), wpwb, row(b_pw[l]),
            row(gla_g[l]), woutb, row(final_g))


LAYER_STACKED = (1, 2, 10, 13)


def kernel(x_prompt, x_sample, cache_conv, state_gla, norm_g, w_in, w_alpha, b_alpha, conv_w,
           conv_b, cn_g, cn_b, w_pw, b_pw, gla_g, w_out, final_g):
    depth = w_in.shape[0]
    n_seq, n_dec = x_prompt.shape[0], x_sample.shape[0]
    state_in = state_gla.reshape(depth, n_dec, D_K, HEAD_V)
    cache_in = jnp.swapaxes(cache_conv, 1, 2)
    winb, wlrt = _input_projection_bf16(w_in)
    wpwb, woutb = w_pw.astype(BF16), w_out.astype(BF16)
    hp, hs = x_prompt, x_sample
    prev_p, prev_s = (), ()
    for l in range(depth):
        lw = _layer_weights(l, winb, wlrt, wpwb, woutb, norm_g, w_alpha, b_alpha, conv_w, conv_b,
                            cn_g, cn_b, b_pw, gla_g, final_g)
        final = l == depth - 1
        hp, *prev_p = _prompt_layer(l, depth, hp, lw, prev_p, final)
        hs, *prev_s = _sample_layer(l, hs, cache_in, state_in, lw, prev_s, final)
    conv_p, gla_p = prev_p
    conv_s, gla_s = prev_s
    return (hp, hs, conv_p, gla_p.reshape(depth, n_seq, HEADS, HEAD_K, HEAD_V),
            jnp.swapaxes(conv_s, 1, 2), gla_s.reshape(depth, n_dec, HEADS, HEAD_K, HEAD_V))
```

```python
import functools

import jax
import jax.numpy as jnp
import numpy as np
from jax import lax
from jax.experimental import pallas as pl
from jax.experimental.pallas import tpu as pltpu

F32 = jnp.float32
BF16 = jnp.bfloat16

D_MODEL = 1024
D_CONV = 512
CONV_GROUPS = 8
CONV_W = 31
HIST = CONV_W - 1
D_V = 512
HEADS = 4
D_K = 256
HEAD_K = 64
HEAD_V = 128
GATE_RANK = 16
GATE_NORM = 16.0
EPS = 1e-6

LANE = 128
SUBLANE = 8
HIST_PAD = 32
N_LANE_GROUPS = D_CONV // LANE
LR_PAD = LANE

OFF_A, OFF_GL, OFF_ZC = 0, 512, 1024
OFF_Q, OFF_K, OFF_V, OFF_ZG, OFF_LR = 1536, 1792, 2048, 2560, 3072
PROJ_COLS = OFF_LR - OFF_ZC

GLA_CHUNK = 128
PROMPT_TILE = 512
PROMPT_SEQS = 1
PROMPT_SEG = PROMPT_TILE // SUBLANE
UHALO0 = SUBLANE
UDATA0 = UHALO0 + HIST_PAD
USEG = UDATA0 + PROMPT_SEG
CSEG = PROMPT_SEG + SUBLANE
assert (USEG // SUBLANE) % 2 == 1 and (CSEG // SUBLANE) % 2 == 1
UTAIL = SUBLANE * USEG - HIST_PAD
SAMPLE_SEQS = 16
SAMPLE_PROJ_COLS = 512
SLAB = SAMPLE_SEQS + SUBLANE
assert (SLAB // SUBLANE) % 2 == 1
CONV_UNROLL = 8
GM_COLS = 256
WPREP_ROWS = 1024
assert OFF_LR % WPREP_ROWS == 0

VMEM_LIMIT = 56 * 1024 * 1024


def _dot(a, b):
    return jnp.dot(a, b, preferred_element_type=F32)


def _dot_nt(a, b):
    return lax.dot_general(a, b, (((1,), (1,)), ((), ())), preferred_element_type=F32)


def _sigmoid(x):
    return 1.0 / (1.0 + jnp.exp(-x))


def _silu(x):
    return x * _sigmoid(x)


def _log_sigmoid(x):
    return jnp.minimum(x, 0.0) - jnp.log(1.0 + jnp.exp(-jnp.abs(x)))


def _rms_rows(x, g):
    ms = jnp.mean(x * x, axis=-1, keepdims=True)
    return x * lax.rsqrt(ms + EPS) * g


def _split3(g):
    hi = g.astype(BF16)
    r1 = g - hi.astype(F32)
    mid = r1.astype(BF16)
    lo = (r1 - mid.astype(F32)).astype(BF16)
    return hi, mid, lo


def _sum_matmul(m, g):
    hi, mid, lo = _split3(g)
    return _dot(m, hi) + _dot(m, mid) + _dot(m, lo)


def _conv_post(cv, zc, cb_ref, cng_ref, cnb_ref, gm_ref, wpw_ref, bpw_ref):
    cv = cv + cb_ref[...]
    gm = gm_ref[...]

    def group_mean(v):
        vb = v.astype(BF16)
        return jnp.concatenate([_dot(vb[:, lo:lo + GM_COLS], gm)
                                for lo in range(0, D_CONV, GM_COLS)], axis=-1)

    mu = group_mean(cv)
    d = cv - mu
    var = group_mean(d * d)
    yn = d * lax.rsqrt(var + EPS) * cng_ref[...] + cnb_ref[...]
    pw = _dot(_silu(yn).astype(BF16), wpw_ref[...]) + bpw_ref[...]
    return pw * _silu(zc)


def _gate_log(hb, wlrt_ref, wal_ref, bal_ref):
    lr = _dot_nt(hb, wlrt_ref[...])
    z = _dot(lr.astype(BF16), wal_ref[...]) + bal_ref[...]
    return _log_sigmoid(z) * (1.0 / GATE_NORM)


def _gla_post(o, zg, gg_ref):
    parts = []
    for h in range(HEADS):
        oh = o[:, h * HEAD_V:(h + 1) * HEAD_V]
        ms = jnp.mean(oh * oh, axis=-1, keepdims=True)
        parts.append(oh * lax.rsqrt(ms + EPS) * gg_ref[...])
    return jnp.concatenate(parts, axis=-1) * _silu(zg)


def _finish(x, cat, wout_ref, fg_ref, final):
    y = x + _dot(cat, wout_ref[...])
    if final:
        y = _rms_rows(y, fg_ref[...])
    return y


def _wprep_kernel(wt_ref, lrt_ref, o_ref, olr_ref):
    o_ref[...] = wt_ref[...].T.astype(BF16)

    @pl.when(pl.program_id(1) == 0)
    def _():
        olr_ref[...] = jnp.zeros_like(olr_ref)
        olr_ref[0:GATE_RANK, :] = lrt_ref[...].astype(BF16)


def _input_projection_bf16(w_in):
    depth = w_in.shape[0]
    wt = jnp.swapaxes(w_in, 1, 2)
    return pl.pallas_call(
        _wprep_kernel,
        grid=(depth, OFF_LR // WPREP_ROWS),
        in_specs=[pl.BlockSpec((None, WPREP_ROWS, D_MODEL), lambda l, c: (l, c, 0)),
                  pl.BlockSpec((None, GATE_RANK, D_MODEL), lambda l, c: (l, OFF_LR // GATE_RANK, 0))],
        out_specs=[pl.BlockSpec((None, D_MODEL, WPREP_ROWS), lambda l, c: (l, 0, c)),
                   pl.BlockSpec((None, LR_PAD, D_MODEL), lambda l, c: (l, 0, 0))],
        out_shape=[jax.ShapeDtypeStruct((depth, D_MODEL, OFF_LR), BF16),
                   jax.ShapeDtypeStruct((depth, LR_PAD, D_MODEL), BF16)],
        compiler_params=pltpu.CompilerParams(dimension_semantics=("arbitrary", "arbitrary")),
        name="weight_layout",
    )(wt, wt)


def _prompt_kernel(x_ref, ng_ref, winb, wlrt_ref, wal_ref, bal_ref, cwb_ref, cb_ref, cng_ref,
                   cnb_ref, gm_ref, wpwb, bpw_ref, gg_ref, woutb, fg_ref, ltri_ref, *rest,
                   ns, tt, final):
    (y_ref, buf_ref, sout_ref, ubuf, cvs, proj, os_, cat, sst) = rest[-9:]
    t = pl.program_id(1)
    seg = tt // SUBLANE
    c = GLA_CHUNK
    half = c // 2

    @pl.when(t == 0)
    def _():
        ubuf[:, :, UTAIL:UTAIL + HIST_PAD, :] = jnp.zeros((ns, N_LANE_GROUPS, HIST_PAD, LANE), F32)
        sst[...] = jnp.zeros_like(sst)

    x = x_ref[...].reshape(ns * tt, D_MODEL)
    hb = _rms_rows(x, ng_ref[...]).astype(BF16)

    a = _dot(hb, winb[:, OFF_A:OFF_A + D_CONV])
    gl = _dot(hb, winb[:, OFF_GL:OFF_GL + D_CONV])
    u = a * _sigmoid(gl)
    for s in range(ns):
        for g in range(N_LANE_GROUPS):
            prev_tail = ubuf[s, g, UTAIL:UTAIL + HIST_PAD, :]
            ext = jnp.concatenate([prev_tail, u[s * tt:(s + 1) * tt, g * LANE:(g + 1) * LANE]], axis=0)
            for j in range(SUBLANE):
                ubuf[s, g, j * USEG + UHALO0:(j + 1) * USEG, :] = ext[j * seg:(j + 1) * seg + HIST_PAD, :]
    gk = _gate_log(hb, wlrt_ref, wal_ref, bal_ref)

    def project(lo, hi):
        proj[:, lo - OFF_ZC:hi - OFF_ZC] = _dot(hb, winb[:, lo:hi])

    project(OFF_Q, OFF_ZG)

    base = UDATA0 - HIST

    def conv_lane_group(g, s):
        w = [cwb_ref[k, :, g * LANE:(g + 1) * LANE] for k in range(CONV_W)]
        for i0 in range(0, seg, CONV_UNROLL):
            accs = [None] * CONV_UNROLL
            for m in range(CONV_UNROLL + CONV_W - 1):
                um = ubuf[s, g, pl.ds(i0 + base + m, SUBLANE, stride=USEG), :]
                for ii in range(CONV_UNROLL):
                    k = m - ii
                    if 0 <= k < CONV_W:
                        accs[ii] = um * w[k] if accs[ii] is None else accs[ii] + um * w[k]
            for ii in range(CONV_UNROLL):
                cvs[s, g, pl.ds(i0 + ii, SUBLANE, stride=CSEG), :] = accs[ii]

    row = lax.broadcasted_iota(jnp.int32, (c, c), 0)
    col = lax.broadcasted_iota(jnp.int32, (c, c), 1)
    causal2 = jnp.concatenate([col <= row] * 2, axis=1)
    ltri = ltri_ref[...]

    def head_diag(top, bot):
        return jnp.concatenate(
            [jnp.concatenate([top, jnp.zeros_like(top)], axis=1),
             jnp.concatenate([jnp.zeros_like(bot), bot], axis=1)], axis=0)

    def attention_block(r0, s_old):
        rows = slice(r0, r0 + c)
        gcum = _sum_matmul(ltri, gk[rows, :])
        q = proj[rows, OFF_Q - OFF_ZC:OFF_K - OFF_ZC] * (HEAD_K ** -0.5)
        k = proj[rows, OFF_K - OFF_ZC:OFF_V - OFF_ZC]
        v = proj[rows, OFF_V - OFF_ZC:OFF_ZG - OFF_ZC].astype(BF16)
        mid = gcum[half - 1:half, :]
        xq = (q * jnp.exp(gcum - mid)).astype(BF16)
        qg = (q * jnp.exp(gcum)).astype(BF16)
        gt = gcum.T
        kt = k.T
        midc = gt[:, half - 1:half]
        glast = gt[:, c - 1:c]
        ykt = (kt * jnp.exp(midc - gt)).astype(BF16)
        kdt = (kt * jnp.exp(glast - gt)).astype(BF16)
        dcol = jnp.exp(glast)
        s_bf = s_old.astype(BF16)
        outs = []
        s_new = []
        for j in range(HEADS // 2):
            pair = slice(j * LANE, (j + 1) * LANE)
            lo = slice(j * LANE, j * LANE + HEAD_K)
            hi = slice(j * LANE + HEAD_K, (j + 1) * LANE)
            v2 = v[:, 2 * j * HEAD_V:(2 * j + 2) * HEAD_V]
            att = _dot(xq[:, pair], head_diag(ykt[lo, :], ykt[hi, :]))
            att = jnp.where(causal2, att, 0.0).astype(BF16)
            outs.append(_dot(att, head_diag(v2[:, 0:HEAD_V], v2[:, HEAD_V:]))
                        + _dot(qg[:, pair], head_diag(s_bf[lo, :], s_bf[hi, :])))
            kv = _dot(kdt[pair, :], v2)
            s_new.append(dcol[lo, :] * s_old[lo, :] + kv[0:HEAD_K, 0:HEAD_V])
            s_new.append(dcol[hi, :] * s_old[hi, :] + kv[HEAD_K:, HEAD_V:])
        os_[rows, :] = jnp.concatenate(outs, axis=-1)
        return jnp.concatenate(s_new, axis=0)

    fill_cols = D_CONV // 2
    fillers = [(lo, lo + fill_cols) for lo in (OFF_ZC, OFF_ZC + fill_cols, OFF_ZG, OFF_ZG + fill_cols)]
    conv_items = [(g, s) for s in range(ns) for g in range(N_LANE_GROUPS)]
    blocks = [(s, i) for s in range(ns) for i in range(tt // c)]
    states = [sst[s] for s in range(ns)]
    for n in range(max(len(conv_items), len(blocks), len(fillers))):
        if n < len(fillers):
            project(*fillers[n])
        if n < len(conv_items):
            conv_lane_group(*conv_items[n])
        if n < len(blocks):
            s, i = blocks[n]
            states[s] = attention_block(s * tt + i * c, states[s])
    for s in range(ns):
        sst[s] = states[s]

    cv = jnp.concatenate(
        [jnp.concatenate(
            [jnp.concatenate([cvs[s, g, j * CSEG:j * CSEG + seg, :] for j in range(SUBLANE)], axis=0)
             for g in range(N_LANE_GROUPS)], axis=-1) for s in range(ns)], axis=0)
    zc = proj[:, 0:OFF_Q - OFF_ZC]
    cvo = _conv_post(cv, zc, cb_ref, cng_ref, cnb_ref, gm_ref, wpwb, bpw_ref)
    cat[:, 0:D_CONV] = cvo.astype(BF16)

    zg = proj[:, OFF_ZG - OFF_ZC:OFF_LR - OFF_ZC]
    cat[:, D_CONV:] = _gla_post(os_[...], zg, gg_ref).astype(BF16)
    y_ref[...] = _finish(x, cat[...], woutb, fg_ref, final).reshape(ns, tt, D_MODEL)

    @pl.when(t == pl.num_programs(1) - 1)
    def _():
        for s in range(ns):
            buf_ref[s] = jnp.concatenate(
                [ubuf[s, g, UTAIL + HIST_PAD - HIST:UTAIL + HIST_PAD, :]
                 for g in range(N_LANE_GROUPS)], axis=-1)
        sout_ref[...] = sst[...]


def _const_spec(a):
    return pl.BlockSpec(a.shape, lambda *_: (0,) * a.ndim, pipeline_mode=pl.Buffered(1))


def _layer_spec(a, l):
    return pl.BlockSpec((None,) + a.shape[1:], lambda *_: (l,) + (0,) * (a.ndim - 1),
                        pipeline_mode=pl.Buffered(1))


def _operands(l, lw, masks, prev):
    ops = list(lw) + list(masks) + list(prev)
    specs = [_layer_spec(a, l) if i in LAYER_STACKED else _const_spec(a) for i, a in enumerate(lw)]
    specs += [_const_spec(m) for m in masks]
    specs += [pl.BlockSpec(memory_space=pl.ANY) for _ in prev]
    return ops, specs


def _prompt_layer(l, depth, x, lw, prev, final):
    b, t, _ = x.shape
    ns, tt = PROMPT_SEQS, PROMPT_TILE
    assert b % ns == 0 and t % tt == 0 and tt % GLA_CHUNK == 0
    assert tt % (SUBLANE * CONV_UNROLL) == 0 and t >= HIST_PAD
    ops, specs = _operands(l, lw, (_lower_tri(GLA_CHUNK),), prev)
    in_specs = [pl.BlockSpec((ns, tt, D_MODEL), lambda i, j: (i, j, 0))] + specs
    out_shape = (jax.ShapeDtypeStruct((b, t, D_MODEL), F32),
                 jax.ShapeDtypeStruct((depth, b, HIST, D_CONV), F32),
                 jax.ShapeDtypeStruct((depth, b, D_K, HEAD_V), F32))
    out_specs = (pl.BlockSpec((ns, tt, D_MODEL), lambda i, j: (i, j, 0)),
                 pl.BlockSpec((None, ns, HIST, D_CONV), lambda i, j: (l, i, 0, 0)),
                 pl.BlockSpec((None, ns, D_K, HEAD_V), lambda i, j: (l, i, 0, 0)))
    n_in = 1 + len(ops)
    aliases = {n_in - len(prev) + k: 1 + k for k in range(len(prev))}
    scratch = [pltpu.VMEM((ns, N_LANE_GROUPS, SUBLANE * USEG, LANE), F32),
               pltpu.VMEM((ns, N_LANE_GROUPS, SUBLANE * CSEG, LANE), F32),
               pltpu.VMEM((ns * tt, PROJ_COLS), F32),
               pltpu.VMEM((ns * tt, D_V), F32),
               pltpu.VMEM((ns * tt, D_MODEL), BF16),
               pltpu.VMEM((ns, D_K, HEAD_V), F32)]
    return pl.pallas_call(
        functools.partial(_prompt_kernel, ns=ns, tt=tt, final=final),
        grid=(b // ns, t // tt),
        in_specs=in_specs, out_specs=out_specs, out_shape=out_shape,
        scratch_shapes=scratch,
        input_output_aliases=aliases,
        compiler_params=pltpu.CompilerParams(
            dimension_semantics=("arbitrary", "arbitrary"), vmem_limit_bytes=VMEM_LIMIT),
        name="prompt_layer",
    )(x, *ops)


def _sample_kernel(x_ref, cache_ref, s0_ref, ng_ref, winb, wlrt_ref, wal_ref, bal_ref, cwb_ref,
                   cb_ref, cng_ref, cnb_ref, gm_ref, wpwb, bpw_ref, gg_ref, woutb, fg_ref,
                   ltri_ref, ball_ref, *rest, nb, ts, final):
    (y_ref, buf_ref, sout_ref, full, cvt, pall, gall) = rest[-7:]
    r = nb * ts
    step = pl.program_id(0)

    @pl.when(step == 0)
    def _():
        xa = x_ref[...].reshape(x_ref.shape[0] * ts, D_MODEL)
        hba = _rms_rows(xa, ng_ref[...]).astype(BF16)
        for lo in range(0, OFF_LR, SAMPLE_PROJ_COLS):
            pall[:, lo:lo + SAMPLE_PROJ_COLS] = _dot(hba, winb[:, lo:lo + SAMPLE_PROJ_COLS])
        gall[...] = _gate_log(hba, wlrt_ref, wal_ref, bal_ref)

    rows = pl.ds(pl.multiple_of(step * r, r), r)
    x = x_ref[pl.ds(pl.multiple_of(step * nb, nb), nb)].reshape(r, D_MODEL)

    a = pall[rows, OFF_A:OFF_A + D_CONV]
    gl = pall[rows, OFF_GL:OFF_GL + D_CONV]
    u = a * _sigmoid(gl)
    for g in range(N_LANE_GROUPS):
        lanes = slice(g * LANE, (g + 1) * LANE)
        for t in range(HIST):
            full[g, t * SLAB:t * SLAB + nb, :] = cache_ref[t, :, lanes]
        for b in range(nb):
            full[g, pl.ds(HIST * SLAB + b, ts, stride=SLAB), :] = u[b * ts:(b + 1) * ts, lanes]
    for g in range(N_LANE_GROUPS):
        w = [cwb_ref[k, :, g * LANE:(g + 1) * LANE] for k in range(CONV_W)]
        for t in range(ts):
            acc = None
            for k in range(CONV_W):
                slab = full[g, (t + k) * SLAB:(t + k) * SLAB + nb, :].reshape(nb // SUBLANE, SUBLANE, LANE)
                acc = slab * w[k] if acc is None else acc + slab * w[k]
            cvt[g, t * SLAB:t * SLAB + nb, :] = acc.reshape(nb, LANE)
    for t in range(HIST):
        buf_ref[t] = jnp.concatenate(
            [full[g, (ts + t) * SLAB:(ts + t) * SLAB + nb, :] for g in range(N_LANE_GROUPS)], axis=-1)
    cv = jnp.concatenate(
        [jnp.concatenate([cvt[g, pl.ds(b, ts, stride=SLAB), :] for b in range(nb)], axis=0)
         for g in range(N_LANE_GROUPS)], axis=-1)
    zc = pall[rows, OFF_ZC:OFF_ZC + D_CONV]
    cvo = _conv_post(cv, zc, cb_ref, cng_ref, cnb_ref, gm_ref, wpwb, bpw_ref)

    q = pall[rows, OFF_Q:OFF_Q + D_K] * (HEAD_K ** -0.5)
    k = pall[rows, OFF_K:OFF_K + D_K]
    v = pall[rows, OFF_V:OFF_V + D_V].astype(BF16)
    g = gall[rows, :]
    gcum = _sum_matmul(ltri_ref[...], g)
    gtot = _sum_matmul(ball_ref[...], g)
    xq = q * jnp.exp(gcum)
    yk = (k * jnp.exp(-gcum)).astype(BF16)
    kdt = (k * jnp.exp(gtot - gcum)).T.astype(BF16)
    gt = g.T

    row = lax.broadcasted_iota(jnp.int32, (r, r), 0)
    col = lax.broadcasted_iota(jnp.int32, (r, r), 1)
    causal = jnp.logical_and(col <= row, (col // ts) == (row // ts))
    head_lo = lax.broadcasted_iota(jnp.int32, (r, LANE), 1) < HEAD_K
    seq_sel = (lax.broadcasted_iota(jnp.int32, (nb, 1, r), 2) // ts
               == lax.broadcasted_iota(jnp.int32, (nb, 1, r), 0))
    blk = (lax.broadcasted_iota(jnp.int32, (r, nb * LANE), 1) // LANE
           == lax.broadcasted_iota(jnp.int32, (r, nb * LANE), 0) // ts)

    s0 = s0_ref[...]
    s0_bf = s0.astype(BF16)
    gsum = jnp.sum(jnp.where(seq_sel, gt[None, :, :], 0.0), axis=-1, keepdims=True)
    s_new = jnp.exp(gsum) * s0

    outs = []
    kvs = []
    for h in range(HEADS):
        j, hi = divmod(h, 2)
        pair = slice(j * LANE, (j + 1) * LANE)
        hm = head_lo if hi == 0 else jnp.logical_not(head_lo)
        xm = jnp.where(hm, xq[:, pair], 0.0).astype(BF16)
        att = jnp.where(causal, _dot_nt(xm, yk[:, pair]), 0.0).astype(BF16)
        vh = v[:, h * HEAD_V:(h + 1) * HEAD_V]
        xblk = jnp.where(blk, jnp.concatenate([xm] * nb, axis=-1), jnp.zeros((), BF16))
        s_pair = s0_bf[:, pair, :].reshape(nb * LANE, HEAD_V)
        outs.append(_dot(att, vh) + _dot(xblk, s_pair))
        kh = kdt[h * HEAD_K:(h + 1) * HEAD_K, :]
        kblk = jnp.where(seq_sel, kh[None, :, :], jnp.zeros((), BF16)).reshape(nb * HEAD_K, r)
        kvs.append(_dot(kblk, vh).reshape(nb, HEAD_K, HEAD_V))
    sout_ref[...] = s_new + jnp.concatenate(kvs, axis=1)

    o = jnp.concatenate(outs, axis=-1)
    zg = pall[rows, OFF_ZG:OFF_ZG + D_V]
    cat = jnp.concatenate([cvo.astype(BF16), _gla_post(o, zg, gg_ref).astype(BF16)], axis=-1)
    y_ref[...] = _finish(x, cat, woutb, fg_ref, final).reshape(nb, ts, D_MODEL)


def _sample_layer(l, x, cache, s0, lw, prev, final):
    b, ts, _ = x.shape
    nb = SAMPLE_SEQS
    assert b % nb == 0 and ts == SUBLANE and nb * ts == LANE
    r = nb * ts
    ops, specs = _operands(l, lw, (_block_tri(r, ts), _block_ones(r, ts)), prev)
    state_specs = [pl.BlockSpec((None, HIST, nb, D_CONV), lambda i: (l, 0, i, 0)),
                   pl.BlockSpec((None, nb, D_K, HEAD_V), lambda i: (l, i, 0, 0))]
    in_specs = [_const_spec(x)] + state_specs + specs
    out_shape = (jax.ShapeDtypeStruct((b, ts, D_MODEL), F32),
                 jax.ShapeDtypeStruct(cache.shape, F32),
                 jax.ShapeDtypeStruct(s0.shape, F32))
    out_specs = [pl.BlockSpec((nb, ts, D_MODEL), lambda i: (i, 0, 0))] + state_specs
    n_in = 3 + len(ops)
    aliases = {n_in - len(prev) + k: 1 + k for k in range(len(prev))}
    return pl.pallas_call(
        functools.partial(_sample_kernel, nb=nb, ts=ts, final=final),
        grid=(b // nb,),
        in_specs=in_specs, out_specs=out_specs, out_shape=out_shape,
        scratch_shapes=[pltpu.VMEM((N_LANE_GROUPS, (HIST + ts) * SLAB, LANE), F32),
                        pltpu.VMEM((N_LANE_GROUPS, ts * SLAB, LANE), F32),
                        pltpu.VMEM((b * ts, OFF_LR), F32),
                        pltpu.VMEM((b * ts, D_K), F32)],
        input_output_aliases=aliases,
        compiler_params=pltpu.CompilerParams(
            dimension_semantics=("arbitrary",), vmem_limit_bytes=VMEM_LIMIT),
        name="sample_layer",
    )(x, cache, s0, *ops)


def _lower_tri(n):
    return jnp.asarray(np.tril(np.ones((n, n), np.float32)), BF16)


def _block_tri(n, blk):
    i = np.arange(n)
    m = (i[None, :] <= i[:, None]) & (i[None, :] // blk == i[:, None] // blk)
    return jnp.asarray(m.astype(np.float32), BF16)


def _block_ones(n, blk):
    i = np.arange(n)
    return jnp.asarray((i[None, :] // blk == i[:, None] // blk).astype(np.float32), BF16)


def _group_mean_matrix():
    gsz = D_CONV // CONV_GROUPS
    assert GM_COLS % gsz == 0 and D_CONV % GM_COLS == 0
    i = np.arange(GM_COLS)
    m = (i[None, :] // gsz == i[:, None] // gsz).astype(np.float32) / gsz
    return jnp.asarray(m, BF16)


def _layer_weights(l, winb, wlrt, wpwb, woutb, norm_g, w_alpha, b_alpha, conv_w, conv_b, cn_g,
                   cn_b, b_pw, gla_g, final_g):
    row = lambda a: a.reshape(1, -1).astype(F32)
    wal = jnp.pad(w_alpha[l], ((0, LR_PAD - GATE_RANK), (0, 0))).astype(BF16)
    cwb = jnp.broadcast_to(conv_w[l][:, None, :], (CONV_W, SUBLANE, D_CONV)).astype(F32)
    return (row(norm_g[l]), winb, wlrt, wal, row(b_alpha[l]), cwb, row(conv_b[l]), row(cn_g[l]),
            row(cn_b[l]), _group_mean_matrix(), wpwb, row(b_pw[l]),
            row(gla_g[l]), woutb, row(final_g))


LAYER_STACKED = (1, 2, 10, 13)


def kernel(x_prompt, x_sample, cache_conv, state_gla, norm_g, w_in, w_alpha, b_alpha, conv_w,
           conv_b, cn_g, cn_b, w_pw, b_pw, gla_g, w_out, final_g):
    depth = w_in.shape[0]
    n_seq, n_dec = x_prompt.shape[0], x_sample.shape[0]
    state_in = state_gla.reshape(depth, n_dec, D_K, HEAD_V)
    cache_in = jnp.swapaxes(cache_conv, 1, 2)
    winb, wlrt = _input_projection_bf16(w_in)
    wpwb, woutb = w_pw.astype(BF16), w_out.astype(BF16)
    hp, hs = x_prompt, x_sample
    prev_p, prev_s = (), ()
    for l in range(depth):
        lw = _layer_weights(l, winb, wlrt, wpwb, woutb, norm_g, w_alpha, b_alpha, conv_w, conv_b,
                            cn_g, cn_b, b_pw, gla_g, final_g)
        final = l == depth - 1
        hp, *prev_p = _prompt_layer(l, depth, hp, lw, prev_p, final)
        hs, *prev_s = _sample_layer(l, hs, cache_in, state_in, lw, prev_s, final)
    conv_p, gla_p = prev_p
    conv_s, gla_s = prev_s
    return (hp, hs, conv_p, gla_p.reshape(depth, n_seq, HEADS, HEAD_K, HEAD_V),
            jnp.swapaxes(conv_s, 1, 2), gla_s.reshape(depth, n_dec, HEADS, HEAD_K, HEAD_V))
```

```python
import functools

import jax
import jax.numpy as jnp
import numpy as np
from jax import lax
from jax.experimental import pallas as pl
from jax.experimental.pallas import tpu as pltpu

F32 = jnp.float32
BF16 = jnp.bfloat16

D_MODEL = 1024
D_CONV = 512
CONV_GROUPS = 8
CONV_W = 31
HIST = CONV_W - 1
D_V = 512
HEADS = 4
D_K = 256
HEAD_K = 64
HEAD_V = 128
GATE_RANK = 16
GATE_NORM = 16.0
EPS = 1e-6

LANE = 128
SUBLANE = 8
HIST_PAD = 32
N_LANE_GROUPS = D_CONV // LANE
LR_PAD = LANE

OFF_A, OFF_GL, OFF_ZC = 0, 512, 1024
OFF_Q, OFF_K, OFF_V, OFF_ZG, OFF_LR = 1536, 1792, 2048, 2560, 3072
PROJ_COLS = OFF_LR - OFF_ZC

GLA_CHUNK = 128
PROMPT_TILE = 1024
PROMPT_SEQS = 1
PROMPT_SEG = PROMPT_TILE // SUBLANE
UHALO0 = SUBLANE
UDATA0 = UHALO0 + HIST_PAD
USEG = UDATA0 + PROMPT_SEG
CSEG = PROMPT_SEG + SUBLANE
assert (USEG // SUBLANE) % 2 == 1 and (CSEG // SUBLANE) % 2 == 1
UTAIL = SUBLANE * USEG - HIST_PAD
SAMPLE_SEQS = 16
SAMPLE_PROJ_COLS = 512
SLAB = SAMPLE_SEQS + SUBLANE
assert (SLAB // SUBLANE) % 2 == 1
CONV_UNROLL = 8
GM_COLS = 256
WPREP_ROWS = 1024
assert OFF_LR % WPREP_ROWS == 0

VMEM_LIMIT = 56 * 1024 * 1024


def _dot(a, b):
    return jnp.dot(a, b, preferred_element_type=F32)


def _dot_nt(a, b):
    return lax.dot_general(a, b, (((1,), (1,)), ((), ())), preferred_element_type=F32)


def _sigmoid(x):
    return 1.0 / (1.0 + jnp.exp(-x))


def _silu(x):
    return x * _sigmoid(x)


def _log_sigmoid(x):
    return jnp.minimum(x, 0.0) - jnp.log(1.0 + jnp.exp(-jnp.abs(x)))


def _rms_rows(x, g):
    ms = jnp.mean(x * x, axis=-1, keepdims=True)
    return x * lax.rsqrt(ms + EPS) * g


def _split3(g):
    hi = g.astype(BF16)
    r1 = g - hi.astype(F32)
    mid = r1.astype(BF16)
    lo = (r1 - mid.astype(F32)).astype(BF16)
    return hi, mid, lo


def _sum_matmul(m, g):
    hi, mid, lo = _split3(g)
    return _dot(m, hi) + _dot(m, mid) + _dot(m, lo)


def _conv_post(cv, zc, cb_ref, cng_ref, cnb_ref, gm_ref, wpw_ref, bpw_ref):
    cv = cv + cb_ref[...]
    gm = gm_ref[...]

    def group_mean(v):
        vb = v.astype(BF16)
        return jnp.concatenate([_dot(vb[:, lo:lo + GM_COLS], gm)
                                for lo in range(0, D_CONV, GM_COLS)], axis=-1)

    mu = group_mean(cv)
    d = cv - mu
    var = group_mean(d * d)
    yn = d * lax.rsqrt(var + EPS) * cng_ref[...] + cnb_ref[...]
    pw = _dot(_silu(yn).astype(BF16), wpw_ref[...]) + bpw_ref[...]
    return pw * _silu(zc)


def _gate_log(hb, wlrt_ref, wal_ref, bal_ref):
    lr = _dot_nt(hb, wlrt_ref[...])
    z = _dot(lr.astype(BF16), wal_ref[...]) + bal_ref[...]
    return _log_sigmoid(z) * (1.0 / GATE_NORM)


def _gla_post(o, zg, gg_ref):
    parts = []
    for h in range(HEADS):
        oh = o[:, h * HEAD_V:(h + 1) * HEAD_V]
        ms = jnp.mean(oh * oh, axis=-1, keepdims=True)
        parts.append(oh * lax.rsqrt(ms + EPS) * gg_ref[...])
    return jnp.concatenate(parts, axis=-1) * _silu(zg)


def _finish(x, cat, wout_ref, fg_ref, final):
    y = x + _dot(cat, wout_ref[...])
    if final:
        y = _rms_rows(y, fg_ref[...])
    return y


def _wprep_kernel(wt_ref, lrt_ref, o_ref, olr_ref):
    o_ref[...] = wt_ref[...].T.astype(BF16)

    @pl.when(pl.program_id(1) == 0)
    def _():
        olr_ref[...] = jnp.zeros_like(olr_ref)
        olr_ref[0:GATE_RANK, :] = lrt_ref[...].astype(BF16)


def _input_projection_bf16(w_in):
    depth = w_in.shape[0]
    wt = jnp.swapaxes(w_in, 1, 2)
    return pl.pallas_call(
        _wprep_kernel,
        grid=(depth, OFF_LR // WPREP_ROWS),
        in_specs=[pl.BlockSpec((None, WPREP_ROWS, D_MODEL), lambda l, c: (l, c, 0)),
                  pl.BlockSpec((None, GATE_RANK, D_MODEL), lambda l, c: (l, OFF_LR // GATE_RANK, 0))],
        out_specs=[pl.BlockSpec((None, D_MODEL, WPREP_ROWS), lambda l, c: (l, 0, c)),
                   pl.BlockSpec((None, LR_PAD, D_MODEL), lambda l, c: (l, 0, 0))],
        out_shape=[jax.ShapeDtypeStruct((depth, D_MODEL, OFF_LR), BF16),
                   jax.ShapeDtypeStruct((depth, LR_PAD, D_MODEL), BF16)],
        compiler_params=pltpu.CompilerParams(dimension_semantics=("arbitrary", "arbitrary")),
        name="weight_layout",
    )(wt, wt)


def _prompt_kernel(x_ref, ng_ref, winb, wlrt_ref, wal_ref, bal_ref, cwb_ref, cb_ref, cng_ref,
                   cnb_ref, gm_ref, wpwb, bpw_ref, gg_ref, woutb, fg_ref, ltri_ref, *rest,
                   ns, tt, final):
    (y_ref, buf_ref, sout_ref, ubuf, cvs, proj, os_, cat, sst) = rest[-9:]
    t = pl.program_id(1)
    seg = tt // SUBLANE
    c = GLA_CHUNK
    half = c // 2

    @pl.when(t == 0)
    def _():
        ubuf[:, :, UTAIL:UTAIL + HIST_PAD, :] = jnp.zeros((ns, N_LANE_GROUPS, HIST_PAD, LANE), F32)
        sst[...] = jnp.zeros_like(sst)

    x = x_ref[...].reshape(ns * tt, D_MODEL)
    hb = _rms_rows(x, ng_ref[...]).astype(BF16)

    a = _dot(hb, winb[:, OFF_A:OFF_A + D_CONV])
    gl = _dot(hb, winb[:, OFF_GL:OFF_GL + D_CONV])
    u = a * _sigmoid(gl)
    for s in range(ns):
        for g in range(N_LANE_GROUPS):
            prev_tail = ubuf[s, g, UTAIL:UTAIL + HIST_PAD, :]
            ext = jnp.concatenate([prev_tail, u[s * tt:(s + 1) * tt, g * LANE:(g + 1) * LANE]], axis=0)
            for j in range(SUBLANE):
                ubuf[s, g, j * USEG + UHALO0:(j + 1) * USEG, :] = ext[j * seg:(j + 1) * seg + HIST_PAD, :]
    gk = _gate_log(hb, wlrt_ref, wal_ref, bal_ref)

    def project(lo, hi):
        proj[:, lo - OFF_ZC:hi - OFF_ZC] = _dot(hb, winb[:, lo:hi])

    project(OFF_Q, OFF_ZG)

    base = UDATA0 - HIST

    def conv_lane_group(g, s):
        w = [cwb_ref[k, :, g * LANE:(g + 1) * LANE] for k in range(CONV_W)]
        for i0 in range(0, seg, CONV_UNROLL):
            accs = [None] * CONV_UNROLL
            for m in range(CONV_UNROLL + CONV_W - 1):
                um = ubuf[s, g, pl.ds(i0 + base + m, SUBLANE, stride=USEG), :]
                for ii in range(CONV_UNROLL):
                    k = m - ii
                    if 0 <= k < CONV_W:
                        accs[ii] = um * w[k] if accs[ii] is None else accs[ii] + um * w[k]
            for ii in range(CONV_UNROLL):
                cvs[s, g, pl.ds(i0 + ii, SUBLANE, stride=CSEG), :] = accs[ii]

    row = lax.broadcasted_iota(jnp.int32, (c, c), 0)
    col = lax.broadcasted_iota(jnp.int32, (c, c), 1)
    causal2 = jnp.concatenate([col <= row] * 2, axis=1)
    ltri = ltri_ref[...]

    def head_diag(top, bot):
        return jnp.concatenate(
            [jnp.concatenate([top, jnp.zeros_like(top)], axis=1),
             jnp.concatenate([jnp.zeros_like(bot), bot], axis=1)], axis=0)

    def attention_block(r0, s_old):
        rows = slice(r0, r0 + c)
        gcum = _sum_matmul(ltri, gk[rows, :])
        q = proj[rows, OFF_Q - OFF_ZC:OFF_K - OFF_ZC] * (HEAD_K ** -0.5)
        k = proj[rows, OFF_K - OFF_ZC:OFF_V - OFF_ZC]
        v = proj[rows, OFF_V - OFF_ZC:OFF_ZG - OFF_ZC].astype(BF16)
        mid = gcum[half - 1:half, :]
        xq = (q * jnp.exp(gcum - mid)).astype(BF16)
        qg = (q * jnp.exp(gcum)).astype(BF16)
        gt = gcum.T
        kt = k.T
        midc = gt[:, half - 1:half]
        glast = gt[:, c - 1:c]
        ykt = (kt * jnp.exp(midc - gt)).astype(BF16)
        kdt = (kt * jnp.exp(glast - gt)).astype(BF16)
        dcol = jnp.exp(glast)
        s_bf = s_old.astype(BF16)
        outs = []
        s_new = []
        for j in range(HEADS // 2):
            pair = slice(j * LANE, (j + 1) * LANE)
            lo = slice(j * LANE, j * LANE + HEAD_K)
            hi = slice(j * LANE + HEAD_K, (j + 1) * LANE)
            v2 = v[:, 2 * j * HEAD_V:(2 * j + 2) * HEAD_V]
            att = _dot(xq[:, pair], head_diag(ykt[lo, :], ykt[hi, :]))
            att = jnp.where(causal2, att, 0.0).astype(BF16)
            outs.append(_dot(att, head_diag(v2[:, 0:HEAD_V], v2[:, HEAD_V:]))
                        + _dot(qg[:, pair], head_diag(s_bf[lo, :], s_bf[hi, :])))
            kv = _dot(kdt[pair, :], v2)
            s_new.append(dcol[lo, :] * s_old[lo, :] + kv[0:HEAD_K, 0:HEAD_V])
            s_new.append(dcol[hi, :] * s_old[hi, :] + kv[HEAD_K:, HEAD_V:])
        os_[rows, :] = jnp.concatenate(outs, axis=-1)
        return jnp.concatenate(s_new, axis=0)

    fill_cols = D_CONV // 2
    fillers = [(lo, lo + fill_cols) for lo in (OFF_ZC, OFF_ZC + fill_cols, OFF_ZG, OFF_ZG + fill_cols)]
    conv_items = [(g, s) for s in range(ns) for g in range(N_LANE_GROUPS)]
    blocks = [(s, i) for s in range(ns) for i in range(tt // c)]
    states = [sst[s] for s in range(ns)]
    for n in range(max(len(conv_items), len(blocks), len(fillers))):
        if n < len(fillers):
            project(*fillers[n])
        if n < len(conv_items):
            conv_lane_group(*conv_items[n])
        if n < len(blocks):
            s, i = blocks[n]
            states[s] = attention_block(s * tt + i * c, states[s])
    for s in range(ns):
        sst[s] = states[s]

    cv = jnp.concatenate(
        [jnp.concatenate(
            [jnp.concatenate([cvs[s, g, j * CSEG:j * CSEG + seg, :] for j in range(SUBLANE)], axis=0)
             for g in range(N_LANE_GROUPS)], axis=-1) for s in range(ns)], axis=0)
    zc = proj[:, 0:OFF_Q - OFF_ZC]
    cvo = _conv_post(cv, zc, cb_ref, cng_ref, cnb_ref, gm_ref, wpwb, bpw_ref)
    cat[:, 0:D_CONV] = cvo.astype(BF16)

    zg = proj[:, OFF_ZG - OFF_ZC:OFF_LR - OFF_ZC]
    cat[:, D_CONV:] = _gla_post(os_[...], zg, gg_ref).astype(BF16)
    y_ref[...] = _finish(x, cat[...], woutb, fg_ref, final).reshape(ns, tt, D_MODEL)

    @pl.when(t == pl.num_programs(1) - 1)
    def _():
        for s in range(ns):
            buf_ref[s] = jnp.concatenate(
                [ubuf[s, g, UTAIL + HIST_PAD - HIST:UTAIL + HIST_PAD, :]
                 for g in range(N_LANE_GROUPS)], axis=-1)
        sout_ref[...] = sst[...]


def _const_spec(a):
    return pl.BlockSpec(a.shape, lambda *_: (0,) * a.ndim, pipeline_mode=pl.Buffered(1))


def _layer_spec(a, l):
    return pl.BlockSpec((None,) + a.shape[1:], lambda *_: (l,) + (0,) * (a.ndim - 1),
                        pipeline_mode=pl.Buffered(1))


def _operands(l, lw, masks, prev):
    ops = list(lw) + list(masks) + list(prev)
    specs = [_layer_spec(a, l) if i in LAYER_STACKED else _const_spec(a) for i, a in enumerate(lw)]
    specs += [_const_spec(m) for m in masks]
    specs += [pl.BlockSpec(memory_space=pl.ANY) for _ in prev]
    return ops, specs


def _prompt_layer(l, depth, x, lw, prev, final):
    b, t, _ = x.shape
    ns, tt = PROMPT_SEQS, PROMPT_TILE
    assert b % ns == 0 and t % tt == 0 and tt % GLA_CHUNK == 0
    assert tt % (SUBLANE * CONV_UNROLL) == 0 and t >= HIST_PAD
    ops, specs = _operands(l, lw, (_lower_tri(GLA_CHUNK),), prev)
    in_specs = [pl.BlockSpec((ns, tt, D_MODEL), lambda i, j: (i, j, 0))] + specs
    out_shape = (jax.ShapeDtypeStruct((b, t, D_MODEL), F32),
                 jax.ShapeDtypeStruct((depth, b, HIST, D_CONV), F32),
                 jax.ShapeDtypeStruct((depth, b, D_K, HEAD_V), F32))
    out_specs = (pl.BlockSpec((ns, tt, D_MODEL), lambda i, j: (i, j, 0)),
                 pl.BlockSpec((None, ns, HIST, D_CONV), lambda i, j: (l, i, 0, 0)),
                 pl.BlockSpec((None, ns, D_K, HEAD_V), lambda i, j: (l, i, 0, 0)))
    n_in = 1 + len(ops)
    aliases = {n_in - len(prev) + k: 1 + k for k in range(len(prev))}
    scratch = [pltpu.VMEM((ns, N_LANE_GROUPS, SUBLANE * USEG, LANE), F32),
               pltpu.VMEM((ns, N_LANE_GROUPS, SUBLANE * CSEG, LANE), F32),
               pltpu.VMEM((ns * tt, PROJ_COLS), F32),
               pltpu.VMEM((ns * tt, D_V), F32),
               pltpu.VMEM((ns * tt, D_MODEL), BF16),
               pltpu.VMEM((ns, D_K, HEAD_V), F32)]
    return pl.pallas_call(
        functools.partial(_prompt_kernel, ns=ns, tt=tt, final=final),
        grid=(b // ns, t // tt),
        in_specs=in_specs, out_specs=out_specs, out_shape=out_shape,
        scratch_shapes=scratch,
        input_output_aliases=aliases,
        compiler_params=pltpu.CompilerParams(
            dimension_semantics=("arbitrary", "arbitrary"), vmem_limit_bytes=VMEM_LIMIT),
        name="prompt_layer",
    )(x, *ops)


def _sample_kernel(x_ref, cache_ref, s0_ref, ng_ref, winb, wlrt_ref, wal_ref, bal_ref, cwb_ref,
                   cb_ref, cng_ref, cnb_ref, gm_ref, wpwb, bpw_ref, gg_ref, woutb, fg_ref,
                   ltri_ref, ball_ref, *rest, nb, ts, final):
    (y_ref, buf_ref, sout_ref, full, cvt, pall, gall) = rest[-7:]
    r = nb * ts
    step = pl.program_id(0)

    @pl.when(step == 0)
    def _():
        xa = x_ref[...].reshape(x_ref.shape[0] * ts, D_MODEL)
        hba = _rms_rows(xa, ng_ref[...]).astype(BF16)
        for lo in range(0, OFF_LR, SAMPLE_PROJ_COLS):
            pall[:, lo:lo + SAMPLE_PROJ_COLS] = _dot(hba, winb[:, lo:lo + SAMPLE_PROJ_COLS])
        gall[...] = _gate_log(hba, wlrt_ref, wal_ref, bal_ref)

    rows = pl.ds(pl.multiple_of(step * r, r), r)
    x = x_ref[pl.ds(pl.multiple_of(step * nb, nb), nb)].reshape(r, D_MODEL)

    a = pall[rows, OFF_A:OFF_A + D_CONV]
    gl = pall[rows, OFF_GL:OFF_GL + D_CONV]
    u = a * _sigmoid(gl)
    for g in range(N_LANE_GROUPS):
        lanes = slice(g * LANE, (g + 1) * LANE)
        for t in range(HIST):
            full[g, t * SLAB:t * SLAB + nb, :] = cache_ref[t, :, lanes]
        for b in range(nb):
            full[g, pl.ds(HIST * SLAB + b, ts, stride=SLAB), :] = u[b * ts:(b + 1) * ts, lanes]
    for g in range(N_LANE_GROUPS):
        w = [cwb_ref[k, :, g * LANE:(g + 1) * LANE] for k in range(CONV_W)]
        for t in range(ts):
            acc = None
            for k in range(CONV_W):
                slab = full[g, (t + k) * SLAB:(t + k) * SLAB + nb, :].reshape(nb // SUBLANE, SUBLANE, LANE)
                acc = slab * w[k] if acc is None else acc + slab * w[k]
            cvt[g, t * SLAB:t * SLAB + nb, :] = acc.reshape(nb, LANE)
    for t in range(HIST):
        buf_ref[t] = jnp.concatenate(
            [full[g, (ts + t) * SLAB:(ts + t) * SLAB + nb, :] for g in range(N_LANE_GROUPS)], axis=-1)
    cv = jnp.concatenate(
        [jnp.concatenate([cvt[g, pl.ds(b, ts, stride=SLAB), :] for b in range(nb)], axis=0)
         for g in range(N_LANE_GROUPS)], axis=-1)
    zc = pall[rows, OFF_ZC:OFF_ZC + D_CONV]
    cvo = _conv_post(cv, zc, cb_ref, cng_ref, cnb_ref, gm_ref, wpwb, bpw_ref)

    q = pall[rows, OFF_Q:OFF_Q + D_K] * (HEAD_K ** -0.5)
    k = pall[rows, OFF_K:OFF_K + D_K]
    v = pall[rows, OFF_V:OFF_V + D_V].astype(BF16)
    g = gall[rows, :]
    gcum = _sum_matmul(ltri_ref[...], g)
    gtot = _sum_matmul(ball_ref[...], g)
    xq = q * jnp.exp(gcum)
    yk = (k * jnp.exp(-gcum)).astype(BF16)
    kdt = (k * jnp.exp(gtot - gcum)).T.astype(BF16)
    gt = g.T

    row = lax.broadcasted_iota(jnp.int32, (r, r), 0)
    col = lax.broadcasted_iota(jnp.int32, (r, r), 1)
    causal = jnp.logical_and(col <= row, (col // ts) == (row // ts))
    head_lo = lax.broadcasted_iota(jnp.int32, (r, LANE), 1) < HEAD_K
    seq_sel = (lax.broadcasted_iota(jnp.int32, (nb, 1, r), 2) // ts
               == lax.broadcasted_iota(jnp.int32, (nb, 1, r), 0))
    blk = (lax.broadcasted_iota(jnp.int32, (r, nb * LANE), 1) // LANE
           == lax.broadcasted_iota(jnp.int32, (r, nb * LANE), 0) // ts)

    s0 = s0_ref[...]
    s0_bf = s0.astype(BF16)
    gsum = jnp.sum(jnp.where(seq_sel, gt[None, :, :], 0.0), axis=-1, keepdims=True)
    s_new = jnp.exp(gsum) * s0

    outs = []
    kvs = []
    for h in range(HEADS):
        j, hi = divmod(h, 2)
        pair = slice(j * LANE, (j + 1) * LANE)
        hm = head_lo if hi == 0 else jnp.logical_not(head_lo)
        xm = jnp.where(hm, xq[:, pair], 0.0).astype(BF16)
        att = jnp.where(causal, _dot_nt(xm, yk[:, pair]), 0.0).astype(BF16)
        vh = v[:, h * HEAD_V:(h + 1) * HEAD_V]
        xblk = jnp.where(blk, jnp.concatenate([xm] * nb, axis=-1), jnp.zeros((), BF16))
        s_pair = s0_bf[:, pair, :].reshape(nb * LANE, HEAD_V)
        outs.append(_dot(att, vh) + _dot(xblk, s_pair))
        kh = kdt[h * HEAD_K:(h + 1) * HEAD_K, :]
        kblk = jnp.where(seq_sel, kh[None, :, :], jnp.zeros((), BF16)).reshape(nb * HEAD_K, r)
        kvs.append(_dot(kblk, vh).reshape(nb, HEAD_K, HEAD_V))
    sout_ref[...] = s_new + jnp.concatenate(kvs, axis=1)

    o = jnp.concatenate(outs, axis=-1)
    zg = pall[rows, OFF_ZG:OFF_ZG + D_V]
    cat = jnp.concatenate([cvo.astype(BF16), _gla_post(o, zg, gg_ref).astype(BF16)], axis=-1)
    y_ref[...] = _finish(x, cat, woutb, fg_ref, final).reshape(nb, ts, D_MODEL)


def _sample_layer(l, x, cache, s0, lw, prev, final):
    b, ts, _ = x.shape
    nb = SAMPLE_SEQS
    assert b % nb == 0 and ts == SUBLANE and nb * ts == LANE
    r = nb * ts
    ops, specs = _operands(l, lw, (_block_tri(r, ts), _block_ones(r, ts)), prev)
    state_specs = [pl.BlockSpec((None, HIST, nb, D_CONV), lambda i: (l, 0, i, 0)),
                   pl.BlockSpec((None, nb, D_K, HEAD_V), lambda i: (l, i, 0, 0))]
    in_specs = [_const_spec(x)] + state_specs + specs
    out_shape = (jax.ShapeDtypeStruct((b, ts, D_MODEL), F32),
                 jax.ShapeDtypeStruct(cache.shape, F32),
                 jax.ShapeDtypeStruct(s0.shape, F32))
    out_specs = [pl.BlockSpec((nb, ts, D_MODEL), lambda i: (i, 0, 0))] + state_specs
    n_in = 3 + len(ops)
    aliases = {n_in - len(prev) + k: 1 + k for k in range(len(prev))}
    return pl.pallas_call(
        functools.partial(_sample_kernel, nb=nb, ts=ts, final=final),
        grid=(b // nb,),
        in_specs=in_specs, out_specs=out_specs, out_shape=out_shape,
        scratch_shapes=[pltpu.VMEM((N_LANE_GROUPS, (HIST + ts) * SLAB, LANE), F32),
                        pltpu.VMEM((N_LANE_GROUPS, ts * SLAB, LANE), F32),
                        pltpu.VMEM((b * ts, OFF_LR), F32),
                        pltpu.VMEM((b * ts, D_K), F32)],
        input_output_aliases=aliases,
        compiler_params=pltpu.CompilerParams(
            dimension_semantics=("arbitrary",), vmem_limit_bytes=VMEM_LIMIT),
        name="sample_layer",
    )(x, cache, s0, *ops)


def _lower_tri(n):
    return jnp.asarray(np.tril(np.ones((n, n), np.float32)), BF16)


def _block_tri(n, blk):
    i = np.arange(n)
    m = (i[None, :] <= i[:, None]) & (i[None, :] // blk == i[:, None] // blk)
    return jnp.asarray(m.astype(np.float32), BF16)


def _block_ones(n, blk):
    i = np.arange(n)
    return jnp.asarray((i[None, :] // blk == i[:, None] // blk).astype(np.float32), BF16)


def _group_mean_matrix():
    gsz = D_CONV // CONV_GROUPS
    assert GM_COLS % gsz == 0 and D_CONV % GM_COLS == 0
    i = np.arange(GM_COLS)
    m = (i[None, :] // gsz == i[:, None] // gsz).astype(np.float32) / gsz
    return jnp.asarray(m, BF16)


def _layer_weights(l, winb, wlrt, wpwb, woutb, norm_g, w_alpha, b_alpha, conv_w, conv_b, cn_g,
                   cn_b, b_pw, gla_g, final_g):
    row = lambda a: a.reshape(1, -1).astype(F32)
    wal = jnp.pad(w_alpha[l], ((0, LR_PAD - GATE_RANK), (0, 0))).astype(BF16)
    cwb = jnp.broadcast_to(conv_w[l][:, None, :], (CONV_W, SUBLANE, D_CONV)).astype(F32)
    return (row(norm_g[l]), winb, wlrt, wal, row(b_alpha[l]), cwb, row(conv_b[l]), row(cn_g[l]),
            row(cn_b[l]), _group_mean_matrix(), wpwb, row(b_pw[l]),
            row(gla_g[l]), woutb, row(final_g))


LAYER_STACKED = (1, 2, 10, 13)


def kernel(x_prompt, x_sample, cache_conv, state_gla, norm_g, w_in, w_alpha, b_alpha, conv_w,
           conv_b, cn_g, cn_b, w_pw, b_pw, gla_g, w_out, final_g):
    depth = w_in.shape[0]
    n_seq, n_dec = x_prompt.shape[0], x_sample.shape[0]
    state_in = state_gla.reshape(depth, n_dec, D_K, HEAD_V)
    cache_in = jnp.swapaxes(cache_conv, 1, 2)
    winb, wlrt = _input_projection_bf16(w_in)
    wpwb, woutb = w_pw.astype(BF16), w_out.astype(BF16)
    hp, hs = x_prompt, x_sample
    prev_p, prev_s = (), ()
    for l in range(depth):
        lw = _layer_weights(l, winb, wlrt, wpwb, woutb, norm_g, w_alpha, b_alpha, conv_w, conv_b,
                            cn_g, cn_b, b_pw, gla_g, final_g)
        final = l == depth - 1
        hp, *prev_p = _prompt_layer(l, depth, hp, lw, prev_p, final)
        hs, *prev_s = _sample_layer(l, hs, cache_in, state_in, lw, prev_s, final)
    conv_p, gla_p = prev_p
    conv_s, gla_s = prev_s
    return (hp, hs, conv_p, gla_p.reshape(depth, n_seq, HEADS, HEAD_K, HEAD_V),
            jnp.swapaxes(conv_s, 1, 2), gla_s.reshape(depth, n_dec, HEADS, HEAD_K, HEAD_V))
```

```python
import functools

import jax
import jax.numpy as jnp
import numpy as np
from jax import lax
from jax.experimental import pallas as pl
from jax.experimental.pallas import tpu as pltpu

F32 = jnp.float32
BF16 = jnp.bfloat16

D_MODEL = 1024
D_CONV = 512
CONV_GROUPS = 8
CONV_W = 31
HIST = CONV_W - 1
D_V = 512
HEADS = 4
D_K = 256
HEAD_K = 64
HEAD_V = 128
GATE_RANK = 16
GATE_NORM = 16.0
EPS = 1e-6

LANE = 128
SUBLANE = 8
HIST_PAD = 32
N_LANE_GROUPS = D_CONV // LANE
LR_PAD = LANE

OFF_A, OFF_GL, OFF_ZC = 0, 512, 1024
OFF_Q, OFF_K, OFF_V, OFF_ZG, OFF_LR = 1536, 1792, 2048, 2560, 3072
PROJ_COLS = OFF_LR - OFF_ZC

GLA_CHUNK = 128
PROMPT_TILE = 512
PROMPT_SEQS = 1
PROMPT_SEG = PROMPT_TILE // SUBLANE
UHALO0 = SUBLANE
UDATA0 = UHALO0 + HIST_PAD
USEG = UDATA0 + PROMPT_SEG
CSEG = PROMPT_SEG + SUBLANE
assert (USEG // SUBLANE) % 2 == 1 and (CSEG // SUBLANE) % 2 == 1
UTAIL = SUBLANE * USEG - HIST_PAD
SAMPLE_SEQS = 16
SAMPLE_PROJ_COLS = 512
SLAB = SAMPLE_SEQS + SUBLANE
assert (SLAB // SUBLANE) % 2 == 1
CONV_UNROLL = 8
GM_COLS = 256
WPREP_ROWS = 1024
assert OFF_LR % WPREP_ROWS == 0

VMEM_LIMIT = 56 * 1024 * 1024


def _dot(a, b):
    return jnp.dot(a, b, preferred_element_type=F32)


def _dot_nt(a, b):
    return lax.dot_general(a, b, (((1,), (1,)), ((), ())), preferred_element_type=F32)


def _sigmoid(x):
    return 1.0 / (1.0 + jnp.exp(-x))


def _silu(x):
    return x * _sigmoid(x)


def _log_sigmoid(x):
    return jnp.minimum(x, 0.0) - jnp.log(1.0 + jnp.exp(-jnp.abs(x)))


def _rms_rows(x, g):
    ms = jnp.mean(x * x, axis=-1, keepdims=True)
    return x * lax.rsqrt(ms + EPS) * g


def _split3(g):
    hi = g.astype(BF16)
    r1 = g - hi.astype(F32)
    mid = r1.astype(BF16)
    lo = (r1 - mid.astype(F32)).astype(BF16)
    return hi, mid, lo


def _sum_matmul(m, g):
    hi, mid, lo = _split3(g)
    return _dot(m, hi) + _dot(m, mid) + _dot(m, lo)


def _conv_post(cv, zc, cb_ref, cng_ref, cnb_ref, gm_ref, wpw_ref, bpw_ref):
    cv = cv + cb_ref[...]
    gm = gm_ref[...]

    def group_mean(v):
        vb = v.astype(BF16)
        return jnp.concatenate([_dot(vb[:, lo:lo + GM_COLS], gm)
                                for lo in range(0, D_CONV, GM_COLS)], axis=-1)

    mu = group_mean(cv)
    d = cv - mu
    var = group_mean(d * d)
    yn = d * lax.rsqrt(var + EPS) * cng_ref[...] + cnb_ref[...]
    pw = _dot(_silu(yn).astype(BF16), wpw_ref[...]) + bpw_ref[...]
    return pw * _silu(zc)


def _gate_log(hb, wlrt_ref, wal_ref, bal_ref):
    lr = _dot_nt(hb, wlrt_ref[...])
    z = _dot(lr.astype(BF16), wal_ref[...]) + bal_ref[...]
    return _log_sigmoid(z) * (1.0 / GATE_NORM)


def _gla_post(o, zg, gg_ref):
    parts = []
    for h in range(HEADS):
        oh = o[:, h * HEAD_V:(h + 1) * HEAD_V]
        ms = jnp.mean(oh * oh, axis=-1, keepdims=True)
        parts.append(oh * lax.rsqrt(ms + EPS) * gg_ref[...])
    return jnp.concatenate(parts, axis=-1) * _silu(zg)


def _finish(x, cat, wout_ref, fg_ref, final):
    y = x + _dot(cat, wout_ref[...])
    if final:
        y = _rms_rows(y, fg_ref[...])
    return y


def _wprep_kernel(wt_ref, lrt_ref, o_ref, olr_ref):
    o_ref[...] = wt_ref[...].T.astype(BF16)

    @pl.when(pl.program_id(1) == 0)
    def _():
        olr_ref[...] = jnp.zeros_like(olr_ref)
        olr_ref[0:GATE_RANK, :] = lrt_ref[...].astype(BF16)


def _input_projection_bf16(w_in):
    depth = w_in.shape[0]
    wt = jnp.swapaxes(w_in, 1, 2)
    return pl.pallas_call(
        _wprep_kernel,
        grid=(depth, OFF_LR // WPREP_ROWS),
        in_specs=[pl.BlockSpec((None, WPREP_ROWS, D_MODEL), lambda l, c: (l, c, 0)),
                  pl.BlockSpec((None, GATE_RANK, D_MODEL), lambda l, c: (l, OFF_LR // GATE_RANK, 0))],
        out_specs=[pl.BlockSpec((None, D_MODEL, WPREP_ROWS), lambda l, c: (l, 0, c)),
                   pl.BlockSpec((None, LR_PAD, D_MODEL), lambda l, c: (l, 0, 0))],
        out_shape=[jax.ShapeDtypeStruct((depth, D_MODEL, OFF_LR), BF16),
                   jax.ShapeDtypeStruct((depth, LR_PAD, D_MODEL), BF16)],
        compiler_params=pltpu.CompilerParams(dimension_semantics=("arbitrary", "arbitrary")),
        name="weight_layout",
    )(wt, wt)


def _prompt_kernel(x_ref, *refs, depth, ns, tt):
    lws = [refs[l * N_LAYER_OPS:(l + 1) * N_LAYER_OPS] for l in range(depth)]
    ltri_ref, y_ref, buf_ref, sout_ref = refs[depth * N_LAYER_OPS:depth * N_LAYER_OPS + 4]
    scratch = refs[depth * N_LAYER_OPS + 4:]
    ubufs, ssts = scratch[0:depth], scratch[depth:2 * depth]
    shared = scratch[2 * depth:]
    t = pl.program_id(1)

    @pl.when(t == 0)
    def _():
        for l in range(depth):
            ubufs[l][:, :, UTAIL:UTAIL + HIST_PAD, :] = jnp.zeros(
                (ns, N_LANE_GROUPS, HIST_PAD, LANE), F32)
            ssts[l][...] = jnp.zeros_like(ssts[l])

    x = x_ref[...].reshape(ns * tt, D_MODEL)
    for l in range(depth):
        x = _prompt_layer_body(x, lws[l], ltri_ref, ubufs[l], ssts[l], shared, ns=ns, tt=tt,
                               final=l == depth - 1)
    y_ref[...] = x.reshape(ns, tt, D_MODEL)

    @pl.when(t == pl.num_programs(1) - 1)
    def _():
        for l in range(depth):
            for s in range(ns):
                buf_ref[l, s] = jnp.concatenate(
                    [ubufs[l][s, g, UTAIL + HIST_PAD - HIST:UTAIL + HIST_PAD, :]
                     for g in range(N_LANE_GROUPS)], axis=-1)
            sout_ref[l] = ssts[l][...]


def _prompt_layer_body(x, lw, ltri_ref, ubuf, sst, shared, *, ns, tt, final):
    (ng_ref, winb, wlrt_ref, wal_ref, bal_ref, cwb_ref, cb_ref, cng_ref, cnb_ref, gm_ref, wpwb,
     bpw_ref, gg_ref, woutb, fg_ref) = lw
    cvs, proj, os_, cat = shared
    seg = tt // SUBLANE
    c = GLA_CHUNK
    half = c // 2

    hb = _rms_rows(x, ng_ref[...]).astype(BF16)

    a = _dot(hb, winb[:, OFF_A:OFF_A + D_CONV])
    gl = _dot(hb, winb[:, OFF_GL:OFF_GL + D_CONV])
    u = a * _sigmoid(gl)
    for s in range(ns):
        for g in range(N_LANE_GROUPS):
            prev_tail = ubuf[s, g, UTAIL:UTAIL + HIST_PAD, :]
            ext = jnp.concatenate([prev_tail, u[s * tt:(s + 1) * tt, g * LANE:(g + 1) * LANE]], axis=0)
            for j in range(SUBLANE):
                ubuf[s, g, j * USEG + UHALO0:(j + 1) * USEG, :] = ext[j * seg:(j + 1) * seg + HIST_PAD, :]
    gk = _gate_log(hb, wlrt_ref, wal_ref, bal_ref)

    def project(lo, hi):
        proj[:, lo - OFF_ZC:hi - OFF_ZC] = _dot(hb, winb[:, lo:hi])

    project(OFF_Q, OFF_ZG)

    base = UDATA0 - HIST

    def conv_lane_group(g, s):
        w = [cwb_ref[k, :, g * LANE:(g + 1) * LANE] for k in range(CONV_W)]
        for i0 in range(0, seg, CONV_UNROLL):
            accs = [None] * CONV_UNROLL
            for m in range(CONV_UNROLL + CONV_W - 1):
                um = ubuf[s, g, pl.ds(i0 + base + m, SUBLANE, stride=USEG), :]
                for ii in range(CONV_UNROLL):
                    k = m - ii
                    if 0 <= k < CONV_W:
                        accs[ii] = um * w[k] if accs[ii] is None else accs[ii] + um * w[k]
            for ii in range(CONV_UNROLL):
                cvs[s, g, pl.ds(i0 + ii, SUBLANE, stride=CSEG), :] = accs[ii]

    row = lax.broadcasted_iota(jnp.int32, (c, c), 0)
    col = lax.broadcasted_iota(jnp.int32, (c, c), 1)
    causal2 = jnp.concatenate([col <= row] * 2, axis=1)
    ltri = ltri_ref[...]

    def head_diag(top, bot):
        return jnp.concatenate(
            [jnp.concatenate([top, jnp.zeros_like(top)], axis=1),
             jnp.concatenate([jnp.zeros_like(bot), bot], axis=1)], axis=0)

    def attention_block(r0, s_old):
        rows = slice(r0, r0 + c)
        gcum = _sum_matmul(ltri, gk[rows, :])
        q = proj[rows, OFF_Q - OFF_ZC:OFF_K - OFF_ZC] * (HEAD_K ** -0.5)
        k = proj[rows, OFF_K - OFF_ZC:OFF_V - OFF_ZC]
        v = proj[rows, OFF_V - OFF_ZC:OFF_ZG - OFF_ZC].astype(BF16)
        mid = gcum[half - 1:half, :]
        xq = (q * jnp.exp(gcum - mid)).astype(BF16)
        qg = (q * jnp.exp(gcum)).astype(BF16)
        gt = gcum.T
        kt = k.T
        midc = gt[:, half - 1:half]
        glast = gt[:, c - 1:c]
        ykt = (kt * jnp.exp(midc - gt)).astype(BF16)
        kdt = (kt * jnp.exp(glast - gt)).astype(BF16)
        dcol = jnp.exp(glast)
        s_bf = s_old.astype(BF16)
        outs = []
        s_new = []
        for j in range(HEADS // 2):
            pair = slice(j * LANE, (j + 1) * LANE)
            lo = slice(j * LANE, j * LANE + HEAD_K)
            hi = slice(j * LANE + HEAD_K, (j + 1) * LANE)
            v2 = v[:, 2 * j * HEAD_V:(2 * j + 2) * HEAD_V]
            att = _dot(xq[:, pair], head_diag(ykt[lo, :], ykt[hi, :]))
            att = jnp.where(causal2, att, 0.0).astype(BF16)
            outs.append(_dot(att, head_diag(v2[:, 0:HEAD_V], v2[:, HEAD_V:]))
                        + _dot(qg[:, pair], head_diag(s_bf[lo, :], s_bf[hi, :])))
            kv = _dot(kdt[pair, :], v2)
            s_new.append(dcol[lo, :] * s_old[lo, :] + kv[0:HEAD_K, 0:HEAD_V])
            s_new.append(dcol[hi, :] * s_old[hi, :] + kv[HEAD_K:, HEAD_V:])
        os_[rows, :] = jnp.concatenate(outs, axis=-1)
        return jnp.concatenate(s_new, axis=0)

    fill_cols = D_CONV // 2
    fillers = [(lo, lo + fill_cols) for lo in (OFF_ZC, OFF_ZC + fill_cols, OFF_ZG, OFF_ZG + fill_cols)]
    conv_items = [(g, s) for s in range(ns) for g in range(N_LANE_GROUPS)]
    blocks = [(s, i) for s in range(ns) for i in range(tt // c)]
    states = [sst[s] for s in range(ns)]
    for n in range(max(len(conv_items), len(blocks), len(fillers))):
        if n < len(fillers):
            project(*fillers[n])
        if n < len(conv_items):
            conv_lane_group(*conv_items[n])
        if n < len(blocks):
            s, i = blocks[n]
            states[s] = attention_block(s * tt + i * c, states[s])
    for s in range(ns):
        sst[s] = states[s]

    cv = jnp.concatenate(
        [jnp.concatenate(
            [jnp.concatenate([cvs[s, g, j * CSEG:j * CSEG + seg, :] for j in range(SUBLANE)], axis=0)
             for g in range(N_LANE_GROUPS)], axis=-1) for s in range(ns)], axis=0)
    zc = proj[:, 0:OFF_Q - OFF_ZC]
    cvo = _conv_post(cv, zc, cb_ref, cng_ref, cnb_ref, gm_ref, wpwb, bpw_ref)
    cat[:, 0:D_CONV] = cvo.astype(BF16)

    zg = proj[:, OFF_ZG - OFF_ZC:OFF_LR - OFF_ZC]
    cat[:, D_CONV:] = _gla_post(os_[...], zg, gg_ref).astype(BF16)
    return _finish(x, cat[...], woutb, fg_ref, final)


def _const_spec(a):
    return pl.BlockSpec(a.shape, lambda *_: (0,) * a.ndim, pipeline_mode=pl.Buffered(1))


def _layer_spec(a, l):
    return pl.BlockSpec((None,) + a.shape[1:], lambda *_: (l,) + (0,) * (a.ndim - 1),
                        pipeline_mode=pl.Buffered(1))


def _operands(l, lw, masks, prev):
    ops = list(lw) + list(masks) + list(prev)
    specs = [_layer_spec(a, l) if i in LAYER_STACKED else _const_spec(a) for i, a in enumerate(lw)]
    specs += [_const_spec(m) for m in masks]
    specs += [pl.BlockSpec(memory_space=pl.ANY) for _ in prev]
    return ops, specs


def _prompt_layers(x, lws):
    b, t, _ = x.shape
    depth = len(lws)
    ns, tt = PROMPT_SEQS, PROMPT_TILE
    assert b % ns == 0 and t % tt == 0 and tt % GLA_CHUNK == 0
    assert tt % (SUBLANE * CONV_UNROLL) == 0 and t >= HIST_PAD
    ops, specs = [], []
    for l, lw in enumerate(lws):
        assert len(lw) == N_LAYER_OPS
        o, sp = _operands(l, lw, (), ())
        ops += o
        specs += sp
    mask = _lower_tri(GLA_CHUNK)
    in_specs = [pl.BlockSpec((ns, tt, D_MODEL), lambda i, j: (i, j, 0))] + specs + [_const_spec(mask)]
    out_shape = (jax.ShapeDtypeStruct((b, t, D_MODEL), F32),
                 jax.ShapeDtypeStruct((depth, b, HIST, D_CONV), F32),
                 jax.ShapeDtypeStruct((depth, b, D_K, HEAD_V), F32))
    out_specs = (pl.BlockSpec((ns, tt, D_MODEL), lambda i, j: (i, j, 0)),
                 pl.BlockSpec((depth, ns, HIST, D_CONV), lambda i, j: (0, i, 0, 0)),
                 pl.BlockSpec((depth, ns, D_K, HEAD_V), lambda i, j: (0, i, 0, 0)))
    ubuf = pltpu.VMEM((ns, N_LANE_GROUPS, SUBLANE * USEG, LANE), F32)
    sst = pltpu.VMEM((ns, D_K, HEAD_V), F32)
    scratch = [ubuf] * depth + [sst] * depth + [
        pltpu.VMEM((ns, N_LANE_GROUPS, SUBLANE * CSEG, LANE), F32),
        pltpu.VMEM((ns * tt, PROJ_COLS), F32),
        pltpu.VMEM((ns * tt, D_V), F32),
        pltpu.VMEM((ns * tt, D_MODEL), BF16)]
    return pl.pallas_call(
        functools.partial(_prompt_kernel, depth=depth, ns=ns, tt=tt),
        grid=(b // ns, t // tt),
        in_specs=in_specs, out_specs=out_specs, out_shape=out_shape,
        scratch_shapes=scratch,
        compiler_params=pltpu.CompilerParams(
            dimension_semantics=("arbitrary", "arbitrary"), vmem_limit_bytes=VMEM_LIMIT),
        name="prompt_layers",
    )(x, *ops, mask)


def _sample_kernel(x_ref, cache_ref, s0_ref, ng_ref, winb, wlrt_ref, wal_ref, bal_ref, cwb_ref,
                   cb_ref, cng_ref, cnb_ref, gm_ref, wpwb, bpw_ref, gg_ref, woutb, fg_ref,
                   ltri_ref, ball_ref, *rest, nb, ts, final):
    (y_ref, buf_ref, sout_ref, full, cvt, pall, gall) = rest[-7:]
    r = nb * ts
    step = pl.program_id(0)

    @pl.when(step == 0)
    def _():
        xa = x_ref[...].reshape(x_ref.shape[0] * ts, D_MODEL)
        hba = _rms_rows(xa, ng_ref[...]).astype(BF16)
        for lo in range(0, OFF_LR, SAMPLE_PROJ_COLS):
            pall[:, lo:lo + SAMPLE_PROJ_COLS] = _dot(hba, winb[:, lo:lo + SAMPLE_PROJ_COLS])
        gall[...] = _gate_log(hba, wlrt_ref, wal_ref, bal_ref)

    rows = pl.ds(pl.multiple_of(step * r, r), r)
    x = x_ref[pl.ds(pl.multiple_of(step * nb, nb), nb)].reshape(r, D_MODEL)

    a = pall[rows, OFF_A:OFF_A + D_CONV]
    gl = pall[rows, OFF_GL:OFF_GL + D_CONV]
    u = a * _sigmoid(gl)
    for g in range(N_LANE_GROUPS):
        lanes = slice(g * LANE, (g + 1) * LANE)
        for t in range(HIST):
            full[g, t * SLAB:t * SLAB + nb, :] = cache_ref[t, :, lanes]
        for b in range(nb):
            full[g, pl.ds(HIST * SLAB + b, ts, stride=SLAB), :] = u[b * ts:(b + 1) * ts, lanes]
    for g in range(N_LANE_GROUPS):
        w = [cwb_ref[k, :, g * LANE:(g + 1) * LANE] for k in range(CONV_W)]
        for t in range(ts):
            acc = None
            for k in range(CONV_W):
                slab = full[g, (t + k) * SLAB:(t + k) * SLAB + nb, :].reshape(nb // SUBLANE, SUBLANE, LANE)
                acc = slab * w[k] if acc is None else acc + slab * w[k]
            cvt[g, t * SLAB:t * SLAB + nb, :] = acc.reshape(nb, LANE)
    for t in range(HIST):
        buf_ref[t] = jnp.concatenate(
            [full[g, (ts + t) * SLAB:(ts + t) * SLAB + nb, :] for g in range(N_LANE_GROUPS)], axis=-1)
    cv = jnp.concatenate(
        [jnp.concatenate([cvt[g, pl.ds(b, ts, stride=SLAB), :] for b in range(nb)], axis=0)
         for g in range(N_LANE_GROUPS)], axis=-1)
    zc = pall[rows, OFF_ZC:OFF_ZC + D_CONV]
    cvo = _conv_post(cv, zc, cb_ref, cng_ref, cnb_ref, gm_ref, wpwb, bpw_ref)

    q = pall[rows, OFF_Q:OFF_Q + D_K] * (HEAD_K ** -0.5)
    k = pall[rows, OFF_K:OFF_K + D_K]
    v = pall[rows, OFF_V:OFF_V + D_V].astype(BF16)
    g = gall[rows, :]
    gcum = _sum_matmul(ltri_ref[...], g)
    gtot = _sum_matmul(ball_ref[...], g)
    xq = q * jnp.exp(gcum)
    yk = (k * jnp.exp(-gcum)).astype(BF16)
    kdt = (k * jnp.exp(gtot - gcum)).T.astype(BF16)
    gt = g.T

    row = lax.broadcasted_iota(jnp.int32, (r, r), 0)
    col = lax.broadcasted_iota(jnp.int32, (r, r), 1)
    causal = jnp.logical_and(col <= row, (col // ts) == (row // ts))
    head_lo = lax.broadcasted_iota(jnp.int32, (r, LANE), 1) < HEAD_K
    seq_sel = (lax.broadcasted_iota(jnp.int32, (nb, 1, r), 2) // ts
               == lax.broadcasted_iota(jnp.int32, (nb, 1, r), 0))
    blk = (lax.broadcasted_iota(jnp.int32, (r, nb * LANE), 1) // LANE
           == lax.broadcasted_iota(jnp.int32, (r, nb * LANE), 0) // ts)

    s0 = s0_ref[...]
    s0_bf = s0.astype(BF16)
    gsum = jnp.sum(jnp.where(seq_sel, gt[None, :, :], 0.0), axis=-1, keepdims=True)
    s_new = jnp.exp(gsum) * s0

    outs = []
    kvs = []
    for h in range(HEADS):
        j, hi = divmod(h, 2)
        pair = slice(j * LANE, (j + 1) * LANE)
        hm = head_lo if hi == 0 else jnp.logical_not(head_lo)
        xm = jnp.where(hm, xq[:, pair], 0.0).astype(BF16)
        att = jnp.where(causal, _dot_nt(xm, yk[:, pair]), 0.0).astype(BF16)
        vh = v[:, h * HEAD_V:(h + 1) * HEAD_V]
        xblk = jnp.where(blk, jnp.concatenate([xm] * nb, axis=-1), jnp.zeros((), BF16))
        s_pair = s0_bf[:, pair, :].reshape(nb * LANE, HEAD_V)
        outs.append(_dot(att, vh) + _dot(xblk, s_pair))
        kh = kdt[h * HEAD_K:(h + 1) * HEAD_K, :]
        kblk = jnp.where(seq_sel, kh[None, :, :], jnp.zeros((), BF16)).reshape(nb * HEAD_K, r)
        kvs.append(_dot(kblk, vh).reshape(nb, HEAD_K, HEAD_V))
    sout_ref[...] = s_new + jnp.concatenate(kvs, axis=1)

    o = jnp.concatenate(outs, axis=-1)
    zg = pall[rows, OFF_ZG:OFF_ZG + D_V]
    cat = jnp.concatenate([cvo.astype(BF16), _gla_post(o, zg, gg_ref).astype(BF16)], axis=-1)
    y_ref[...] = _finish(x, cat, woutb, fg_ref, final).reshape(nb, ts, D_MODEL)


def _sample_layer(l, x, cache, s0, lw, prev, final):
    b, ts, _ = x.shape
    nb = SAMPLE_SEQS
    assert b % nb == 0 and ts == SUBLANE and nb * ts == LANE
    r = nb * ts
    ops, specs = _operands(l, lw, (_block_tri(r, ts), _block_ones(r, ts)), prev)
    state_specs = [pl.BlockSpec((None, HIST, nb, D_CONV), lambda i: (l, 0, i, 0)),
                   pl.BlockSpec((None, nb, D_K, HEAD_V), lambda i: (l, i, 0, 0))]
    in_specs = [_const_spec(x)] + state_specs + specs
    out_shape = (jax.ShapeDtypeStruct((b, ts, D_MODEL), F32),
                 jax.ShapeDtypeStruct(cache.shape, F32),
                 jax.ShapeDtypeStruct(s0.shape, F32))
    out_specs = [pl.BlockSpec((nb, ts, D_MODEL), lambda i: (i, 0, 0))] + state_specs
    n_in = 3 + len(ops)
    aliases = {n_in - len(prev) + k: 1 + k for k in range(len(prev))}
    return pl.pallas_call(
        functools.partial(_sample_kernel, nb=nb, ts=ts, final=final),
        grid=(b // nb,),
        in_specs=in_specs, out_specs=out_specs, out_shape=out_shape,
        scratch_shapes=[pltpu.VMEM((N_LANE_GROUPS, (HIST + ts) * SLAB, LANE), F32),
                        pltpu.VMEM((N_LANE_GROUPS, ts * SLAB, LANE), F32),
                        pltpu.VMEM((b * ts, OFF_LR), F32),
                        pltpu.VMEM((b * ts, D_K), F32)],
        input_output_aliases=aliases,
        compiler_params=pltpu.CompilerParams(
            dimension_semantics=("arbitrary",), vmem_limit_bytes=VMEM_LIMIT),
        name="sample_layer",
    )(x, cache, s0, *ops)


def _lower_tri(n):
    return jnp.asarray(np.tril(np.ones((n, n), np.float32)), BF16)


def _block_tri(n, blk):
    i = np.arange(n)
    m = (i[None, :] <= i[:, None]) & (i[None, :] // blk == i[:, None] // blk)
    return jnp.asarray(m.astype(np.float32), BF16)


def _block_ones(n, blk):
    i = np.arange(n)
    return jnp.asarray((i[None, :] // blk == i[:, None] // blk).astype(np.float32), BF16)


def _group_mean_matrix():
    gsz = D_CONV // CONV_GROUPS
    assert GM_COLS % gsz == 0 and D_CONV % GM_COLS == 0
    i = np.arange(GM_COLS)
    m = (i[None, :] // gsz == i[:, None] // gsz).astype(np.float32) / gsz
    return jnp.asarray(m, BF16)


def _layer_weights(l, winb, wlrt, wpwb, woutb, norm_g, w_alpha, b_alpha, conv_w, conv_b, cn_g,
                   cn_b, b_pw, gla_g, final_g):
    row = lambda a: a.reshape(1, -1).astype(F32)
    wal = jnp.pad(w_alpha[l], ((0, LR_PAD - GATE_RANK), (0, 0))).astype(BF16)
    cwb = jnp.broadcast_to(conv_w[l][:, None, :], (CONV_W, SUBLANE, D_CONV)).astype(F32)
    return (row(norm_g[l]), winb, wlrt, wal, row(b_alpha[l]), cwb, row(conv_b[l]), row(cn_g[l]),
            row(cn_b[l]), _group_mean_matrix(), wpwb, row(b_pw[l]),
            row(gla_g[l]), woutb, row(final_g))


LAYER_STACKED = (1, 2, 10, 13)
N_LAYER_OPS = 15


def kernel(x_prompt, x_sample, cache_conv, state_gla, norm_g, w_in, w_alpha, b_alpha, conv_w,
           conv_b, cn_g, cn_b, w_pw, b_pw, gla_g, w_out, final_g):
    depth = w_in.shape[0]
    n_seq, n_dec = x_prompt.shape[0], x_sample.shape[0]
    state_in = state_gla.reshape(depth, n_dec, D_K, HEAD_V)
    cache_in = jnp.swapaxes(cache_conv, 1, 2)
    winb, wlrt = _input_projection_bf16(w_in)
    wpwb, woutb = w_pw.astype(BF16), w_out.astype(BF16)
    lws = [_layer_weights(l, winb, wlrt, wpwb, woutb, norm_g, w_alpha, b_alpha, conv_w, conv_b,
                          cn_g, cn_b, b_pw, gla_g, final_g) for l in range(depth)]
    hp, conv_p, gla_p = _prompt_layers(x_prompt, lws)
    hs = x_sample
    prev_s = ()
    for l in range(depth):
        hs, *prev_s = _sample_layer(l, hs, cache_in, state_in, lws[l], prev_s, l == depth - 1)
    conv_s, gla_s = prev_s
    return (hp, hs, conv_p, gla_p.reshape(depth, n_seq, HEADS, HEAD_K, HEAD_V),
            jnp.swapaxes(conv_s, 1, 2), gla_s.reshape(depth, n_dec, HEADS, HEAD_K, HEAD_V))
```

```python
import functools

import jax
import jax.numpy as jnp
import numpy as np
from jax import lax
from jax.experimental import pallas as pl
from jax.experimental.pallas import tpu as pltpu

F32 = jnp.float32
BF16 = jnp.bfloat16

D_MODEL = 1024
D_CONV = 512
CONV_GROUPS = 8
CONV_W = 31
HIST = CONV_W - 1
D_V = 512
HEADS = 4
D_K = 256
HEAD_K = 64
HEAD_V = 128
GATE_RANK = 16
GATE_NORM = 16.0
EPS = 1e-6

LANE = 128
SUBLANE = 8
HIST_PAD = 32
N_LANE_GROUPS = D_CONV // LANE
LR_PAD = LANE

OFF_A, OFF_GL, OFF_ZC = 0, 512, 1024
OFF_Q, OFF_K, OFF_V, OFF_ZG, OFF_LR = 1536, 1792, 2048, 2560, 3072
PROJ_COLS = OFF_LR - OFF_ZC

GLA_CHUNK = 128
PROMPT_TILE = 512
PROMPT_SEQS = 1
PROMPT_SEG = PROMPT_TILE // SUBLANE
UHALO0 = SUBLANE
UDATA0 = UHALO0 + HIST_PAD
USEG = UDATA0 + PROMPT_SEG
CSEG = PROMPT_SEG + SUBLANE
assert (USEG // SUBLANE) % 2 == 1 and (CSEG // SUBLANE) % 2 == 1
UTAIL = SUBLANE * USEG - HIST_PAD
SAMPLE_SEQS = 16
SAMPLE_PROJ_COLS = 512
SLAB = SAMPLE_SEQS + SUBLANE
assert (SLAB // SUBLANE) % 2 == 1
CONV_UNROLL = 8
GM_COLS = 256
WPREP_ROWS = 1024
assert OFF_LR % WPREP_ROWS == 0

VMEM_LIMIT = 56 * 1024 * 1024


def _dot(a, b):
    return jnp.dot(a, b, preferred_element_type=F32)


def _dot_nt(a, b):
    return lax.dot_general(a, b, (((1,), (1,)), ((), ())), preferred_element_type=F32)


def _sigmoid(x):
    return 1.0 / (1.0 + jnp.exp(-x))


def _silu(x):
    return x * _sigmoid(x)


def _log_sigmoid(x):
    return jnp.minimum(x, 0.0) - jnp.log(1.0 + jnp.exp(-jnp.abs(x)))


def _rms_rows(x, g):
    ms = jnp.mean(x * x, axis=-1, keepdims=True)
    return x * lax.rsqrt(ms + EPS) * g


def _split3(g):
    hi = g.astype(BF16)
    r1 = g - hi.astype(F32)
    mid = r1.astype(BF16)
    lo = (r1 - mid.astype(F32)).astype(BF16)
    return hi, mid, lo


def _sum_matmul(m, g):
    hi, mid, lo = _split3(g)
    return _dot(m, hi) + _dot(m, mid) + _dot(m, lo)


def _conv_post(cv, zc, cb_ref, cng_ref, cnb_ref, gm_ref, wpw_ref, bpw_ref):
    gm = gm_ref[...]
    parts = []
    for lo in range(0, D_CONV, GM_COLS):
        cols = slice(lo, lo + GM_COLS)
        cvh = cv[:, cols] + cb_ref[:, cols]
        mu = _dot(cvh.astype(BF16), gm)
        d = cvh - mu
        var = _dot((d * d).astype(BF16), gm)
        yn = d * lax.rsqrt(var + EPS) * cng_ref[:, cols] + cnb_ref[:, cols]
        parts.append(_silu(yn).astype(BF16))
    pw = _dot(jnp.concatenate(parts, axis=-1), wpw_ref[...]) + bpw_ref[...]
    return pw * _silu(zc)


def _gate_log(hb, wlrt_ref, wal_ref, bal_ref):
    lr = _dot_nt(hb, wlrt_ref[...])
    z = _dot(lr.astype(BF16), wal_ref[...]) + bal_ref[...]
    return _log_sigmoid(z) * (1.0 / GATE_NORM)


def _gla_post(o, zg, gg_ref):
    parts = []
    for h in range(HEADS):
        oh = o[:, h * HEAD_V:(h + 1) * HEAD_V]
        ms = jnp.mean(oh * oh, axis=-1, keepdims=True)
        parts.append(oh * lax.rsqrt(ms + EPS) * gg_ref[...])
    return jnp.concatenate(parts, axis=-1) * _silu(zg)


def _finish(x, cat, wout_ref, fg_ref, final):
    y = x + _dot(cat, wout_ref[...])
    if final:
        y = _rms_rows(y, fg_ref[...])
    return y


def _wprep_kernel(wt_ref, lrt_ref, o_ref, olr_ref):
    o_ref[...] = wt_ref[...].T.astype(BF16)

    @pl.when(pl.program_id(1) == 0)
    def _():
        olr_ref[...] = jnp.zeros_like(olr_ref)
        olr_ref[0:GATE_RANK, :] = lrt_ref[...].astype(BF16)


def _input_projection_bf16(w_in):
    depth = w_in.shape[0]
    wt = jnp.swapaxes(w_in, 1, 2)
    return pl.pallas_call(
        _wprep_kernel,
        grid=(depth, OFF_LR // WPREP_ROWS),
        in_specs=[pl.BlockSpec((None, WPREP_ROWS, D_MODEL), lambda l, c: (l, c, 0)),
                  pl.BlockSpec((None, GATE_RANK, D_MODEL), lambda l, c: (l, OFF_LR // GATE_RANK, 0))],
        out_specs=[pl.BlockSpec((None, D_MODEL, WPREP_ROWS), lambda l, c: (l, 0, c)),
                   pl.BlockSpec((None, LR_PAD, D_MODEL), lambda l, c: (l, 0, 0))],
        out_shape=[jax.ShapeDtypeStruct((depth, D_MODEL, OFF_LR), BF16),
                   jax.ShapeDtypeStruct((depth, LR_PAD, D_MODEL), BF16)],
        compiler_params=pltpu.CompilerParams(dimension_semantics=("arbitrary", "arbitrary")),
        name="weight_layout",
    )(wt, wt)


def _prompt_kernel(x_ref, ng_ref, winb, wlrt_ref, wal_ref, bal_ref, cwb_ref, cb_ref, cng_ref,
                   cnb_ref, gm_ref, wpwb, bpw_ref, gg_ref, woutb, fg_ref, ltri_ref, *rest,
                   ns, tt, final):
    (y_ref, buf_ref, sout_ref, ubuf, cvs, proj, os_, cat, sst) = rest[-9:]
    t = pl.program_id(1)
    seg = tt // SUBLANE
    c = GLA_CHUNK
    half = c // 2

    @pl.when(t == 0)
    def _():
        ubuf[:, :, UTAIL:UTAIL + HIST_PAD, :] = jnp.zeros((ns, N_LANE_GROUPS, HIST_PAD, LANE), F32)
        sst[...] = jnp.zeros_like(sst)

    x = x_ref[...].reshape(ns * tt, D_MODEL)
    hb = _rms_rows(x, ng_ref[...]).astype(BF16)

    a = _dot(hb, winb[:, OFF_A:OFF_A + D_CONV])
    gl = _dot(hb, winb[:, OFF_GL:OFF_GL + D_CONV])
    u = a * _sigmoid(gl)
    for s in range(ns):
        for g in range(N_LANE_GROUPS):
            prev_tail = ubuf[s, g, UTAIL:UTAIL + HIST_PAD, :]
            ext = jnp.concatenate([prev_tail, u[s * tt:(s + 1) * tt, g * LANE:(g + 1) * LANE]], axis=0)
            for j in range(SUBLANE):
                ubuf[s, g, j * USEG + UHALO0:(j + 1) * USEG, :] = ext[j * seg:(j + 1) * seg + HIST_PAD, :]
    gk = _gate_log(hb, wlrt_ref, wal_ref, bal_ref)

    def project(lo, hi):
        proj[:, lo - OFF_ZC:hi - OFF_ZC] = _dot(hb, winb[:, lo:hi])

    project(OFF_Q, OFF_ZG)

    base = UDATA0 - HIST

    def conv_lane_group(g, s):
        w = [cwb_ref[k, :, g * LANE:(g + 1) * LANE] for k in range(CONV_W)]
        for i0 in range(0, seg, CONV_UNROLL):
            accs = [None] * CONV_UNROLL
            for m in range(CONV_UNROLL + CONV_W - 1):
                um = ubuf[s, g, pl.ds(i0 + base + m, SUBLANE, stride=USEG), :]
                for ii in range(CONV_UNROLL):
                    k = m - ii
                    if 0 <= k < CONV_W:
                        accs[ii] = um * w[k] if accs[ii] is None else accs[ii] + um * w[k]
            for ii in range(CONV_UNROLL):
                cvs[s, g, pl.ds(i0 + ii, SUBLANE, stride=CSEG), :] = accs[ii]

    row = lax.broadcasted_iota(jnp.int32, (c, c), 0)
    col = lax.broadcasted_iota(jnp.int32, (c, c), 1)
    causal2 = jnp.concatenate([col <= row] * 2, axis=1)
    ltri = ltri_ref[...]

    def head_diag(top, bot):
        return jnp.concatenate(
            [jnp.concatenate([top, jnp.zeros_like(top)], axis=1),
             jnp.concatenate([jnp.zeros_like(bot), bot], axis=1)], axis=0)

    def attention_block(r0, s_old):
        rows = slice(r0, r0 + c)
        gcum = _sum_matmul(ltri, gk[rows, :])
        q = proj[rows, OFF_Q - OFF_ZC:OFF_K - OFF_ZC] * (HEAD_K ** -0.5)
        k = proj[rows, OFF_K - OFF_ZC:OFF_V - OFF_ZC]
        v = proj[rows, OFF_V - OFF_ZC:OFF_ZG - OFF_ZC].astype(BF16)
        mid = gcum[half - 1:half, :]
        xq = (q * jnp.exp(gcum - mid)).astype(BF16)
        qg = (q * jnp.exp(gcum)).astype(BF16)
        gt = gcum.T
        kt = k.T
        midc = gt[:, half - 1:half]
        glast = gt[:, c - 1:c]
        ykt = (kt * jnp.exp(midc - gt)).astype(BF16)
        kdt = (kt * jnp.exp(glast - gt)).astype(BF16)
        dcol = jnp.exp(glast)
        s_bf = s_old.astype(BF16)
        outs = []
        s_new = []
        for j in range(HEADS // 2):
            pair = slice(j * LANE, (j + 1) * LANE)
            lo = slice(j * LANE, j * LANE + HEAD_K)
            hi = slice(j * LANE + HEAD_K, (j + 1) * LANE)
            v2 = v[:, 2 * j * HEAD_V:(2 * j + 2) * HEAD_V]
            att = _dot(xq[:, pair], head_diag(ykt[lo, :], ykt[hi, :]))
            att = jnp.where(causal2, att, 0.0).astype(BF16)
            outs.append(_dot(att, head_diag(v2[:, 0:HEAD_V], v2[:, HEAD_V:]))
                        + _dot(qg[:, pair], head_diag(s_bf[lo, :], s_bf[hi, :])))
            kv = _dot(kdt[pair, :], v2)
            s_new.append(dcol[lo, :] * s_old[lo, :] + kv[0:HEAD_K, 0:HEAD_V])
            s_new.append(dcol[hi, :] * s_old[hi, :] + kv[HEAD_K:, HEAD_V:])
        os_[rows, :] = jnp.concatenate(outs, axis=-1)
        return jnp.concatenate(s_new, axis=0)

    fill_cols = D_CONV // 2
    fillers = [(lo, lo + fill_cols) for lo in (OFF_ZC, OFF_ZC + fill_cols, OFF_ZG, OFF_ZG + fill_cols)]
    conv_items = [(g, s) for s in range(ns) for g in range(N_LANE_GROUPS)]
    blocks = [(s, i) for s in range(ns) for i in range(tt // c)]
    states = [sst[s] for s in range(ns)]
    for n in range(max(len(conv_items), len(blocks), len(fillers))):
        if n < len(fillers):
            project(*fillers[n])
        if n < len(conv_items):
            conv_lane_group(*conv_items[n])
        if n < len(blocks):
            s, i = blocks[n]
            states[s] = attention_block(s * tt + i * c, states[s])
    for s in range(ns):
        sst[s] = states[s]

    cv = jnp.concatenate(
        [jnp.concatenate(
            [jnp.concatenate([cvs[s, g, j * CSEG:j * CSEG + seg, :] for j in range(SUBLANE)], axis=0)
             for g in range(N_LANE_GROUPS)], axis=-1) for s in range(ns)], axis=0)
    zc = proj[:, 0:OFF_Q - OFF_ZC]
    cvo = _conv_post(cv, zc, cb_ref, cng_ref, cnb_ref, gm_ref, wpwb, bpw_ref)
    cat[:, 0:D_CONV] = cvo.astype(BF16)

    zg = proj[:, OFF_ZG - OFF_ZC:OFF_LR - OFF_ZC]
    cat[:, D_CONV:] = _gla_post(os_[...], zg, gg_ref).astype(BF16)
    y_ref[...] = _finish(x, cat[...], woutb, fg_ref, final).reshape(ns, tt, D_MODEL)

    @pl.when(t == pl.num_programs(1) - 1)
    def _():
        for s in range(ns):
            buf_ref[s] = jnp.concatenate(
                [ubuf[s, g, UTAIL + HIST_PAD - HIST:UTAIL + HIST_PAD, :]
                 for g in range(N_LANE_GROUPS)], axis=-1)
        sout_ref[...] = sst[...]


def _const_spec(a):
    return pl.BlockSpec(a.shape, lambda *_: (0,) * a.ndim, pipeline_mode=pl.Buffered(1))


def _layer_spec(a, l):
    return pl.BlockSpec((None,) + a.shape[1:], lambda *_: (l,) + (0,) * (a.ndim - 1),
                        pipeline_mode=pl.Buffered(1))


def _operands(l, lw, masks, prev):
    ops = list(lw) + list(masks) + list(prev)
    specs = [_layer_spec(a, l) if i in LAYER_STACKED else _const_spec(a) for i, a in enumerate(lw)]
    specs += [_const_spec(m) for m in masks]
    specs += [pl.BlockSpec(memory_space=pl.ANY) for _ in prev]
    return ops, specs


def _prompt_layer(l, depth, x, lw, prev, final):
    b, t, _ = x.shape
    ns, tt = PROMPT_SEQS, PROMPT_TILE
    assert b % ns == 0 and t % tt == 0 and tt % GLA_CHUNK == 0
    assert tt % (SUBLANE * CONV_UNROLL) == 0 and t >= HIST_PAD
    ops, specs = _operands(l, lw, (_lower_tri(GLA_CHUNK),), prev)
    in_specs = [pl.BlockSpec((ns, tt, D_MODEL), lambda i, j: (i, j, 0))] + specs
    out_shape = (jax.ShapeDtypeStruct((b, t, D_MODEL), F32),
                 jax.ShapeDtypeStruct((depth, b, HIST, D_CONV), F32),
                 jax.ShapeDtypeStruct((depth, b, D_K, HEAD_V), F32))
    out_specs = (pl.BlockSpec((ns, tt, D_MODEL), lambda i, j: (i, j, 0)),
                 pl.BlockSpec((None, ns, HIST, D_CONV), lambda i, j: (l, i, 0, 0)),
                 pl.BlockSpec((None, ns, D_K, HEAD_V), lambda i, j: (l, i, 0, 0)))
    n_in = 1 + len(ops)
    aliases = {n_in - len(prev) + k: 1 + k for k in range(len(prev))}
    scratch = [pltpu.VMEM((ns, N_LANE_GROUPS, SUBLANE * USEG, LANE), F32),
               pltpu.VMEM((ns, N_LANE_GROUPS, SUBLANE * CSEG, LANE), F32),
               pltpu.VMEM((ns * tt, PROJ_COLS), F32),
               pltpu.VMEM((ns * tt, D_V), F32),
               pltpu.VMEM((ns * tt, D_MODEL), BF16),
               pltpu.VMEM((ns, D_K, HEAD_V), F32)]
    return pl.pallas_call(
        functools.partial(_prompt_kernel, ns=ns, tt=tt, final=final),
        grid=(b // ns, t // tt),
        in_specs=in_specs, out_specs=out_specs, out_shape=out_shape,
        scratch_shapes=scratch,
        input_output_aliases=aliases,
        compiler_params=pltpu.CompilerParams(
            dimension_semantics=("arbitrary", "arbitrary"), vmem_limit_bytes=VMEM_LIMIT),
        name="prompt_layer",
    )(x, *ops)


def _sample_kernel(x_ref, cache_ref, s0_ref, ng_ref, winb, wlrt_ref, wal_ref, bal_ref, cwb_ref,
                   cb_ref, cng_ref, cnb_ref, gm_ref, wpwb, bpw_ref, gg_ref, woutb, fg_ref,
                   ltri_ref, ball_ref, *rest, nb, ts, final):
    (y_ref, buf_ref, sout_ref, full, cvt, pall, gall) = rest[-7:]
    r = nb * ts
    step = pl.program_id(0)

    @pl.when(step == 0)
    def _():
        xa = x_ref[...].reshape(x_ref.shape[0] * ts, D_MODEL)
        hba = _rms_rows(xa, ng_ref[...]).astype(BF16)
        for lo in range(0, OFF_LR, SAMPLE_PROJ_COLS):
            pall[:, lo:lo + SAMPLE_PROJ_COLS] = _dot(hba, winb[:, lo:lo + SAMPLE_PROJ_COLS])
        gall[...] = _gate_log(hba, wlrt_ref, wal_ref, bal_ref)

    rows = pl.ds(pl.multiple_of(step * r, r), r)
    x = x_ref[pl.ds(pl.multiple_of(step * nb, nb), nb)].reshape(r, D_MODEL)

    a = pall[rows, OFF_A:OFF_A + D_CONV]
    gl = pall[rows, OFF_GL:OFF_GL + D_CONV]
    u = a * _sigmoid(gl)
    for g in range(N_LANE_GROUPS):
        lanes = slice(g * LANE, (g + 1) * LANE)
        for t in range(HIST):
            full[g, t * SLAB:t * SLAB + nb, :] = cache_ref[t, :, lanes]
        for b in range(nb):
            full[g, pl.ds(HIST * SLAB + b, ts, stride=SLAB), :] = u[b * ts:(b + 1) * ts, lanes]
    for g in range(N_LANE_GROUPS):
        w = [cwb_ref[k, :, g * LANE:(g + 1) * LANE] for k in range(CONV_W)]
        for t in range(ts):
            acc = None
            for k in range(CONV_W):
                slab = full[g, (t + k) * SLAB:(t + k) * SLAB + nb, :].reshape(nb // SUBLANE, SUBLANE, LANE)
                acc = slab * w[k] if acc is None else acc + slab * w[k]
            cvt[g, t * SLAB:t * SLAB + nb, :] = acc.reshape(nb, LANE)
    for t in range(HIST):
        buf_ref[t] = jnp.concatenate(
            [full[g, (ts + t) * SLAB:(ts + t) * SLAB + nb, :] for g in range(N_LANE_GROUPS)], axis=-1)
    cv = jnp.concatenate(
        [jnp.concatenate([cvt[g, pl.ds(b, ts, stride=SLAB), :] for b in range(nb)], axis=0)
         for g in range(N_LANE_GROUPS)], axis=-1)
    zc = pall[rows, OFF_ZC:OFF_ZC + D_CONV]
    cvo = _conv_post(cv, zc, cb_ref, cng_ref, cnb_ref, gm_ref, wpwb, bpw_ref)

    q = pall[rows, OFF_Q:OFF_Q + D_K] * (HEAD_K ** -0.5)
    k = pall[rows, OFF_K:OFF_K + D_K]
    v = pall[rows, OFF_V:OFF_V + D_V].astype(BF16)
    g = gall[rows, :]
    gcum = _sum_matmul(ltri_ref[...], g)
    gtot = _sum_matmul(ball_ref[...], g)
    xq = q * jnp.exp(gcum)
    yk = (k * jnp.exp(-gcum)).astype(BF16)
    kdt = (k * jnp.exp(gtot - gcum)).T.astype(BF16)
    gt = g.T

    row = lax.broadcasted_iota(jnp.int32, (r, r), 0)
    col = lax.broadcasted_iota(jnp.int32, (r, r), 1)
    causal = jnp.logical_and(col <= row, (col // ts) == (row // ts))
    head_lo = lax.broadcasted_iota(jnp.int32, (r, LANE), 1) < HEAD_K
    seq_sel = (lax.broadcasted_iota(jnp.int32, (nb, 1, r), 2) // ts
               == lax.broadcasted_iota(jnp.int32, (nb, 1, r), 0))
    blk = (lax.broadcasted_iota(jnp.int32, (r, nb * LANE), 1) // LANE
           == lax.broadcasted_iota(jnp.int32, (r, nb * LANE), 0) // ts)

    s0 = s0_ref[...]
    s0_bf = s0.astype(BF16)
    gsum = jnp.sum(jnp.where(seq_sel, gt[None, :, :], 0.0), axis=-1, keepdims=True)
    s_new = jnp.exp(gsum) * s0

    outs = []
    kvs = []
    for h in range(HEADS):
        j, hi = divmod(h, 2)
        pair = slice(j * LANE, (j + 1) * LANE)
        hm = head_lo if hi == 0 else jnp.logical_not(head_lo)
        xm = jnp.where(hm, xq[:, pair], 0.0).astype(BF16)
        att = jnp.where(causal, _dot_nt(xm, yk[:, pair]), 0.0).astype(BF16)
        vh = v[:, h * HEAD_V:(h + 1) * HEAD_V]
        xblk = jnp.where(blk, jnp.concatenate([xm] * nb, axis=-1), jnp.zeros((), BF16))
        s_pair = s0_bf[:, pair, :].reshape(nb * LANE, HEAD_V)
        outs.append(_dot(att, vh) + _dot(xblk, s_pair))
        kh = kdt[h * HEAD_K:(h + 1) * HEAD_K, :]
        kblk = jnp.where(seq_sel, kh[None, :, :], jnp.zeros((), BF16)).reshape(nb * HEAD_K, r)
        kvs.append(_dot(kblk, vh).reshape(nb, HEAD_K, HEAD_V))
    sout_ref[...] = s_new + jnp.concatenate(kvs, axis=1)

    o = jnp.concatenate(outs, axis=-1)
    zg = pall[rows, OFF_ZG:OFF_ZG + D_V]
    cat = jnp.concatenate([cvo.astype(BF16), _gla_post(o, zg, gg_ref).astype(BF16)], axis=-1)
    y_ref[...] = _finish(x, cat, woutb, fg_ref, final).reshape(nb, ts, D_MODEL)


def _sample_layer(l, x, cache, s0, lw, prev, final):
    b, ts, _ = x.shape
    nb = SAMPLE_SEQS
    assert b % nb == 0 and ts == SUBLANE and nb * ts == LANE
    r = nb * ts
    ops, specs = _operands(l, lw, (_block_tri(r, ts), _block_ones(r, ts)), prev)
    state_specs = [pl.BlockSpec((None, HIST, nb, D_CONV), lambda i: (l, 0, i, 0)),
                   pl.BlockSpec((None, nb, D_K, HEAD_V), lambda i: (l, i, 0, 0))]
    in_specs = [_const_spec(x)] + state_specs + specs
    out_shape = (jax.ShapeDtypeStruct((b, ts, D_MODEL), F32),
                 jax.ShapeDtypeStruct(cache.shape, F32),
                 jax.ShapeDtypeStruct(s0.shape, F32))
    out_specs = [pl.BlockSpec((nb, ts, D_MODEL), lambda i: (i, 0, 0))] + state_specs
    n_in = 3 + len(ops)
    aliases = {n_in - len(prev) + k: 1 + k for k in range(len(prev))}
    return pl.pallas_call(
        functools.partial(_sample_kernel, nb=nb, ts=ts, final=final),
        grid=(b // nb,),
        in_specs=in_specs, out_specs=out_specs, out_shape=out_shape,
        scratch_shapes=[pltpu.VMEM((N_LANE_GROUPS, (HIST + ts) * SLAB, LANE), F32),
                        pltpu.VMEM((N_LANE_GROUPS, ts * SLAB, LANE), F32),
                        pltpu.VMEM((b * ts, OFF_LR), F32),
                        pltpu.VMEM((b * ts, D_K), F32)],
        input_output_aliases=aliases,
        compiler_params=pltpu.CompilerParams(
            dimension_semantics=("arbitrary",), vmem_limit_bytes=VMEM_LIMIT),
        name="sample_layer",
    )(x, cache, s0, *ops)


def _lower_tri(n):
    return jnp.asarray(np.tril(np.ones((n, n), np.float32)), BF16)


def _block_tri(n, blk):
    i = np.arange(n)
    m = (i[None, :] <= i[:, None]) & (i[None, :] // blk == i[:, None] // blk)
    return jnp.asarray(m.astype(np.float32), BF16)


def _block_ones(n, blk):
    i = np.arange(n)
    return jnp.asarray((i[None, :] // blk == i[:, None] // blk).astype(np.float32), BF16)


def _group_mean_matrix():
    gsz = D_CONV // CONV_GROUPS
    assert GM_COLS % gsz == 0 and D_CONV % GM_COLS == 0
    i = np.arange(GM_COLS)
    m = (i[None, :] // gsz == i[:, None] // gsz).astype(np.float32) / gsz
    return jnp.asarray(m, BF16)


def _layer_weights(l, winb, wlrt, wpwb, woutb, norm_g, w_alpha, b_alpha, conv_w, conv_b, cn_g,
                   cn_b, b_pw, gla_g, final_g):
    row = lambda a: a.reshape(1, -1).astype(F32)
    wal = jnp.pad(w_alpha[l], ((0, LR_PAD - GATE_RANK), (0, 0))).astype(BF16)
    cwb = jnp.broadcast_to(conv_w[l][:, None, :], (CONV_W, SUBLANE, D_CONV)).astype(F32)
    return (row(norm_g[l]), winb, wlrt, wal, row(b_alpha[l]), cwb, row(conv_b[l]), row(cn_g[l]),
            row(cn_b[l]), _group_mean_matrix(), wpwb, row(b_pw[l]),
            row(gla_g[l]), woutb, row(final_g))


LAYER_STACKED = (1, 2, 10, 13)


def kernel(x_prompt, x_sample, cache_conv, state_gla, norm_g, w_in, w_alpha, b_alpha, conv_w,
           conv_b, cn_g, cn_b, w_pw, b_pw, gla_g, w_out, final_g):
    depth = w_in.shape[0]
    n_seq, n_dec = x_prompt.shape[0], x_sample.shape[0]
    state_in = state_gla.reshape(depth, n_dec, D_K, HEAD_V)
    cache_in = jnp.swapaxes(cache_conv, 1, 2)
    winb, wlrt = _input_projection_bf16(w_in)
    wpwb, woutb = w_pw.astype(BF16), w_out.astype(BF16)
    hp, hs = x_prompt, x_sample
    prev_p, prev_s = (), ()
    for l in range(depth):
        lw = _layer_weights(l, winb, wlrt, wpwb, woutb, norm_g, w_alpha, b_alpha, conv_w, conv_b,
                            cn_g, cn_b, b_pw, gla_g, final_g)
        final = l == depth - 1
        hp, *prev_p = _prompt_layer(l, depth, hp, lw, prev_p, final)
        hs, *prev_s = _sample_layer(l, hs, cache_in, state_in, lw, prev_s, final)
    conv_p, gla_p = prev_p
    conv_s, gla_s = prev_s
    return (hp, hs, conv_p, gla_p.reshape(depth, n_seq, HEADS, HEAD_K, HEAD_V),
            jnp.swapaxes(conv_s, 1, 2), gla_s.reshape(depth, n_dec, HEADS, HEAD_K, HEAD_V))
```
